```python
import math
import jax, jax.numpy as jnp
from jax import lax
import numpy as np

D_MODEL = 4096
BATCH = 4
SEQ = 2048
DEPTH = 2
DEC_BATCH = 8
DEC_SEQ = 1
PAST_LEN = 16384
PAGE_SIZE = 128

N_HEADS = 16
HEAD_DIM = 128
D_ATTN = N_HEADS * HEAD_DIM
Q_BLOCK = 128
ATTN_SCALE = HEAD_DIM ** -0.5
FORGET_BIAS_INIT = 3.0
NEG_INF = -1e30
D_CONV = D_MODEL // 4
CONV_WIDTH = 31
D_SSM = D_MODEL // 4
SSM_GROUP = 16
N_GROUPS = D_SSM // SSM_GROUP
N_STATE = 64
DT_MIN = 1e-3
DT_MAX = 1e-1
D_FF = -(-8 * D_MODEL // (3 * 256)) * 256
EPS = 1e-6
N_BRANCH = 3

OFF_GLU = 0
OFF_Q = OFF_GLU + 2 * D_CONV
OFF_K = OFF_Q + D_ATTN
OFF_V = OFF_K + D_ATTN
OFF_F = OFF_V + D_ATTN
OFF_U = OFF_F + N_HEADS
OFF_G = OFF_U + D_SSM
N_IN = OFF_G + N_BRANCH * D_MODEL

kernel_name = 'hybrid_conv_fox_s5_decoder_step'


def rms_norm(x, g):
    xf = x.astype(jnp.float32)
    y = xf * lax.rsqrt(jnp.mean(xf * xf, axis=-1, keepdims=True) + EPS)
    return (y * g.astype(jnp.float32)).astype(x.dtype)


def layer_norm(x, g, b):
    xf = x.astype(jnp.float32)
    mu = jnp.mean(xf, axis=-1, keepdims=True)
    xc = xf - mu
    y = xc * lax.rsqrt(jnp.mean(xc * xc, axis=-1, keepdims=True) + EPS)
    return (y * g.astype(jnp.float32) + b.astype(jnp.float32)).astype(x.dtype)


def causal_depthwise_conv(x_ext, w, b):
    out = lax.conv_general_dilated(
        x_ext, w[:, None, :].astype(x_ext.dtype), window_strides=(1,), padding='VALID',
        dimension_numbers=('NWC', 'WIO', 'NWC'), feature_group_count=x_ext.shape[-1])
    return out + b.astype(out.dtype)


def fox_logits(q, fq, k, fk):
    s = jnp.einsum('bqhd,bkhd->bhqk', q, k, preferred_element_type=jnp.float32) * ATTN_SCALE
    return s + jnp.swapaxes(fq, 1, 2)[..., :, None] - jnp.swapaxes(fk, 1, 2)[..., None, :]


def fox_attention_prompt(q, k, v, logf):
    b, n, h, dh = q.shape
    cum = jnp.cumsum(logf, axis=1)
    nb = n // Q_BLOCK
    q_blocks = jnp.swapaxes(q.reshape(b, nb, Q_BLOCK, h, dh), 0, 1)
    c_blocks = jnp.swapaxes(cum.reshape(b, nb, Q_BLOCK, h), 0, 1)
    k_pos = jnp.arange(n)

    def block(args):
        i, qi, ci = args
        s = fox_logits(qi, ci, k, cum)
        q_pos = i * Q_BLOCK + jnp.arange(Q_BLOCK)
        s = jnp.where(k_pos[None, :] <= q_pos[:, None], s, NEG_INF)
        p = jax.nn.softmax(s, axis=-1).astype(v.dtype)
        return jnp.einsum('bhqk,bkhd->bqhd', p, v)

    o = lax.map(block, (jnp.arange(nb), q_blocks, c_blocks))
    return jnp.swapaxes(o, 0, 1).reshape(b, n, h * dh)


def fox_attention_sample(q, k, v, logf, k_past, v_past, logf_past):
    b, n, h, dh = q.shape
    p_len = k_past.shape[1]
    cum = jnp.cumsum(jnp.concatenate([logf_past.astype(jnp.float32), logf], axis=1), axis=1)
    cum_past, cum_new = cum[:, :p_len], cum[:, p_len:]
    s_past = fox_logits(q, cum_new, k_past.astype(q.dtype), cum_past)
    s_new = fox_logits(q, cum_new, k, cum_new)
    causal = jnp.arange(n)[None, :] <= jnp.arange(n)[:, None]
    s_new = jnp.where(causal, s_new, NEG_INF)
    p = jax.nn.softmax(jnp.concatenate([s_past, s_new], axis=-1), axis=-1).astype(v.dtype)
    o = (jnp.einsum('bhqk,bkhd->bqhd', p[..., :p_len], v_past.astype(v.dtype))
         + jnp.einsum('bhqk,bkhd->bqhd', p[..., p_len:], v))
    return o.reshape(b, n, h * dh)


def s5_scan(u, h0_re, h0_im, lam_re, lam_im, log_dt, b_re, b_im, c_re, c_im, d_skip):
    f32 = jnp.float32
    lam = lax.complex(lam_re.astype(f32), lam_im.astype(f32))
    dt = jnp.exp(log_dt.astype(f32))[:, None]
    lam_bar = jnp.exp(lam * dt)
    b_mat = lax.complex(b_re.astype(f32), b_im.astype(f32))
    b_bar = ((lam_bar - 1.0) / lam)[..., None] * b_mat
    uf = u.astype(f32)
    bu = jnp.einsum('blgp,gnp->blgn', uf, b_bar)
    h0 = lax.complex(h0_re.astype(f32), h0_im.astype(f32))
    bu = bu.at[:, 0].add(lam_bar * h0)
    a = jnp.broadcast_to(lam_bar, bu.shape)

    def combine(e1, e2):
        a1, x1 = e1
        a2, x2 = e2
        return a2 * a1, a2 * x1 + x2

    _, hs = lax.associative_scan(combine, (a, bu), axis=1)
    c_mat = lax.complex(c_re.astype(f32), c_im.astype(f32))
    y = jnp.real(jnp.einsum('blgn,gpn->blgp', hs, c_mat)) + d_skip.astype(f32) * uf
    h_last = hs[:, -1]
    return y.astype(u.dtype), jnp.real(h_last), jnp.imag(h_last)


def gather_pages(pool, page_table):
    g = jnp.take(pool, page_table, axis=0)
    return g.reshape((page_table.shape[0], page_table.shape[1] * pool.shape[1]) + pool.shape[2:])


def setup_inputs(seed: int = 0) -> dict:
    key = jax.random.key(seed)
    ks = list(jax.random.split(key, 40))
    f32 = jnp.float32
    n_pages = PAST_LEN // PAGE_SIZE
    n_pool = (DEC_BATCH * n_pages * 5) // 4

    def nrm(k, shape, scale):
        return jax.random.normal(k, shape, f32) * scale

    def gain(k, shape):
        return 1.0 + nrm(k, shape, 0.02)

    perm = jax.random.permutation(ks[8], n_pool)
    page_table = perm[:DEC_BATCH * n_pages].reshape(DEC_BATCH, n_pages).astype(jnp.int32)
    lam_im_base = jnp.pi * jnp.arange(N_STATE, dtype=f32)
    return {
        'x_prompt': nrm(ks[0], (BATCH, SEQ, D_MODEL), 1.0),
        'x_sample': nrm(ks[1], (DEC_BATCH, DEC_SEQ, D_MODEL), 1.0),
        'cache_k': nrm(ks[2], (DEPTH, n_pool, PAGE_SIZE, N_HEADS, HEAD_DIM), 1.0),
        'cache_v': nrm(ks[3], (DEPTH, n_pool, PAGE_SIZE, N_HEADS, HEAD_DIM), 1.0),
        'cache_logf': jax.nn.log_sigmoid(nrm(ks[4], (DEPTH, n_pool, PAGE_SIZE, N_HEADS), 1.0) + FORGET_BIAS_INIT),
        'state_conv': nrm(ks[5], (DEPTH, DEC_BATCH, CONV_WIDTH - 1, D_CONV), 0.5),
        'state_ssm_re': nrm(ks[6], (DEPTH, DEC_BATCH, N_GROUPS, N_STATE), 0.5),
        'state_ssm_im': nrm(ks[7], (DEPTH, DEC_BATCH, N_GROUPS, N_STATE), 0.5),
        'page_table': page_table,
        'g_mix_pre': gain(ks[9], (DEPTH, D_MODEL)),
        'w_in': nrm(ks[10], (DEPTH, D_MODEL, N_IN), D_MODEL ** -0.5),
        'b_forget': FORGET_BIAS_INIT + nrm(ks[11], (DEPTH, N_HEADS), 0.5),
        'conv_w': nrm(ks[12], (DEPTH, CONV_WIDTH, D_CONV), CONV_WIDTH ** -0.5),
        'conv_b': nrm(ks[13], (DEPTH, D_CONV), 0.02),
        'ln_conv_g': gain(ks[14], (DEPTH, D_CONV)),
        'ln_conv_b': nrm(ks[15], (DEPTH, D_CONV), 0.02),
        'ssm_lambda_re': -0.5 + nrm(ks[16], (DEPTH, N_GROUPS, N_STATE), 0.01),
        'ssm_lambda_im': lam_im_base + nrm(ks[17], (DEPTH, N_GROUPS, N_STATE), 0.01),
        'ssm_log_dt': jax.random.uniform(ks[18], (DEPTH, N_GROUPS), f32, math.log(DT_MIN), math.log(DT_MAX)),
        'ssm_b_re': nrm(ks[19], (DEPTH, N_GROUPS, N_STATE, SSM_GROUP), (2 * SSM_GROUP) ** -0.5),
        'ssm_b_im': nrm(ks[20], (DEPTH, N_GROUPS, N_STATE, SSM_GROUP), (2 * SSM_GROUP) ** -0.5),
        'ssm_c_re': nrm(ks[21], (DEPTH, N_GROUPS, SSM_GROUP, N_STATE), (2 * N_STATE) ** -0.5),
        'ssm_c_im': nrm(ks[22], (DEPTH, N_GROUPS, SSM_GROUP, N_STATE), (2 * N_STATE) ** -0.5),
        'ssm_d': nrm(ks[23], (DEPTH, N_GROUPS, SSM_GROUP), 1.0),
        'w_glu': nrm(ks[24], (DEPTH, D_SSM, D_SSM), D_SSM ** -0.5),
        'b_glu': nrm(ks[25], (DEPTH, D_SSM), 0.02),
        'w_branch_conv': nrm(ks[26], (DEPTH, D_CONV, D_MODEL), D_CONV ** -0.5),
        'w_branch_attn': nrm(ks[27], (DEPTH, D_ATTN, D_MODEL), D_ATTN ** -0.5),
        'w_branch_ssm': nrm(ks[28], (DEPTH, D_SSM, D_MODEL), D_SSM ** -0.5),
        'w_out': nrm(ks[29], (DEPTH, D_MODEL, D_MODEL), D_MODEL ** -0.5),
        'g_mix_post': gain(ks[30], (DEPTH, D_MODEL)),
        'g_ffn_pre': gain(ks[31], (DEPTH, D_MODEL)),
        'w_ffn_gate': nrm(ks[32], (DEPTH, D_MODEL, D_FF), D_MODEL ** -0.5),
        'w_ffn_up': nrm(ks[33], (DEPTH, D_MODEL, D_FF), D_MODEL ** -0.5),
        'w_ffn_down': nrm(ks[34], (DEPTH, D_FF, D_MODEL), D_FF ** -0.5),
        'g_ffn_post': gain(ks[35], (DEPTH, D_MODEL)),
    }


def reference(x_prompt, x_sample, cache_k, cache_v, cache_logf, state_conv, state_ssm_re, state_ssm_im,
              page_table, g_mix_pre, w_in, b_forget, conv_w, conv_b, ln_conv_g, ln_conv_b,
              ssm_lambda_re, ssm_lambda_im, ssm_log_dt, ssm_b_re, ssm_b_im, ssm_c_re, ssm_c_im, ssm_d,
              w_glu, b_glu, w_branch_conv, w_branch_attn, w_branch_ssm, w_out, g_mix_post,
              g_ffn_pre, w_ffn_gate, w_ffn_up, w_ffn_down, g_ffn_post):

    def run_layer(x, l, conv_prev, h0_re, h0_im, past):
        b, n, _ = x.shape
        h = rms_norm(x, g_mix_pre[l])
        z = h @ w_in[l]
        a = z[..., OFF_GLU:OFF_GLU + D_CONV] * jax.nn.sigmoid(z[..., OFF_GLU + D_CONV:OFF_Q])
        a_ext = jnp.concatenate([conv_prev.astype(a.dtype), a], axis=1)
        conv_new = a_ext[:, a_ext.shape[1] - (CONV_WIDTH - 1):]
        a = causal_depthwise_conv(a_ext, conv_w[l], conv_b[l])
        a = jax.nn.silu(layer_norm(a, ln_conv_g[l], ln_conv_b[l]))
        br_a = a @ w_branch_conv[l]
        q = z[..., OFF_Q:OFF_K].reshape(b, n, N_HEADS, HEAD_DIM)
        k = z[..., OFF_K:OFF_V].reshape(b, n, N_HEADS, HEAD_DIM)
        v = z[..., OFF_V:OFF_F].reshape(b, n, N_HEADS, HEAD_DIM)
        logf = jax.nn.log_sigmoid(z[..., OFF_F:OFF_U].astype(jnp.float32) + b_forget[l].astype(jnp.float32))
        if past is None:
            o = fox_attention_prompt(q, k, v, logf)
        else:
            o = fox_attention_sample(q, k, v, logf, *past)
        br_b = o @ w_branch_attn[l]
        u = z[..., OFF_U:OFF_G].reshape(b, n, N_GROUPS, SSM_GROUP)
        y, h_re, h_im = s5_scan(u, h0_re, h0_im, ssm_lambda_re[l], ssm_lambda_im[l], ssm_log_dt[l],
                                ssm_b_re[l], ssm_b_im[l], ssm_c_re[l], ssm_c_im[l], ssm_d[l])
        y = jax.nn.gelu(y.reshape(b, n, D_SSM))
        y = y * jax.nn.sigmoid(y @ w_glu[l] + b_glu[l])
        br_c = y @ w_branch_ssm[l]
        gates = jax.nn.sigmoid(z[..., OFF_G:].reshape(b, n, N_BRANCH, D_MODEL))
        m = gates[..., 0, :] * br_a + gates[..., 1, :] * br_b + gates[..., 2, :] * br_c
        x = x + rms_norm(m @ w_out[l], g_mix_post[l])
        hf = rms_norm(x, g_ffn_pre[l])
        f = (jax.nn.silu(hf @ w_ffn_gate[l]) * (hf @ w_ffn_up[l])) @ w_ffn_down[l]
        x = x + rms_norm(f, g_ffn_post[l])
        return x, k, v, logf, conv_new, h_re, h_im

    y_p = x_prompt
    y_s = x_sample
    kp, vp, fp, cp, srp, sip = [], [], [], [], [], []
    kd, vd, fd, cd, srd, sid = [], [], [], [], [], []
    bp = x_prompt.shape[0]
    for l in range(DEPTH):
        conv0 = jnp.zeros((bp, CONV_WIDTH - 1, D_CONV), x_prompt.dtype)
        h0 = jnp.zeros((bp, N_GROUPS, N_STATE), jnp.float32)
        y_p, k_l, v_l, f_l, c_l, sr_l, si_l = run_layer(y_p, l, conv0, h0, h0, None)
        kp.append(k_l); vp.append(v_l); fp.append(f_l); cp.append(c_l); srp.append(sr_l); sip.append(si_l)
        past = (gather_pages(cache_k[l], page_table), gather_pages(cache_v[l], page_table),
                gather_pages(cache_logf[l], page_table))
        y_s, k_l, v_l, f_l, c_l, sr_l, si_l = run_layer(y_s, l, state_conv[l], state_ssm_re[l],
                                                          state_ssm_im[l], past)
        kd.append(k_l); vd.append(v_l); fd.append(f_l); cd.append(c_l); srd.append(sr_l); sid.append(si_l)

    k_prompt = jnp.stack(kp)
    v_prompt = jnp.stack(vp)
    logf_prompt = jnp.stack(fp)
    conv_prompt = jnp.stack(cp)
    ssm_re_prompt = jnp.stack(srp)
    ssm_im_prompt = jnp.stack(sip)
    k_sample = jnp.stack(kd)
    v_sample = jnp.stack(vd)
    logf_sample = jnp.stack(fd)
    conv_sample = jnp.stack(cd)
    ssm_re_sample = jnp.stack(srd)
    ssm_im_sample = jnp.stack(sid)
    return (y_p, y_s, k_prompt, v_prompt, logf_prompt, conv_prompt, ssm_re_prompt, ssm_im_prompt,
            k_sample, v_sample, logf_sample, conv_sample, ssm_re_sample, ssm_im_sample)
```

```python
import functools
import math

import jax
import jax.numpy as jnp
from jax import lax
from jax.experimental import pallas as pl
from jax.experimental.pallas import tpu as pltpu

f32 = jnp.float32
bf16 = jnp.bfloat16
SDS = jax.ShapeDtypeStruct

LANES = 128
SUBLANES = 8
VMEM_LIMIT_BYTES = 56 * 2 ** 20
EPS = 1e-6
NEG_INF = -1e30
CONV_HIST = 32


def _cparams(*sem):
    return pltpu.CompilerParams(dimension_semantics=sem, vmem_limit_bytes=VMEM_LIMIT_BYTES)


def _sigmoid(x):
    return 1.0 / (1.0 + jnp.exp(-x))


def _tile(n, pref):
    if n <= pref:
        return n
    t = pref
    while n % t:
        t //= 2
    return t


def _rms(x, g):
    return x * lax.rsqrt(jnp.mean(x * x, axis=-1, keepdims=True) + EPS) * g


def _rmsnorm_kernel(x_ref, g_ref, o_ref):
    o_ref[...] = _rms(x_ref[...], g_ref[...]).astype(o_ref.dtype)


def _rmsnorm(x, g):
    m, d = x.shape
    tm = _tile(m, 256)
    return pl.pallas_call(
        _rmsnorm_kernel, out_shape=SDS((m, d), bf16), grid=(m // tm,),
        in_specs=[pl.BlockSpec((tm, d), lambda i: (i, 0)), pl.BlockSpec((1, d), lambda i: (0, 0))],
        out_specs=pl.BlockSpec((tm, d), lambda i: (i, 0)),
        compiler_params=_cparams("parallel"), name="rmsnorm")(x, g.reshape(1, d))


def _mm_kernel(x_ref, w_ref, *o_refs):
    acc = jnp.dot(x_ref[...].astype(bf16), w_ref[...].astype(bf16), preferred_element_type=f32)
    for o in o_refs:
        o[...] = acc.astype(o.dtype)


def _mm(x, w, out_dtypes, col0=0, ncols=None, tm=1024, tn=512, name="mm"):
    m, k = x.shape
    ncols = w.shape[1] - col0 if ncols is None else ncols
    tm = _tile(m, tm)
    tn = _tile(ncols, tn)
    assert col0 % tn == 0 and ncols % tn == 0
    cb = col0 // tn
    outs = pl.pallas_call(
        _mm_kernel, out_shape=[SDS((m, ncols), dt) for dt in out_dtypes],
        grid=(m // tm, ncols // tn),
        in_specs=[pl.BlockSpec((tm, k), lambda i, j: (i, 0), pipeline_mode=pl.Buffered(1)),
                  pl.BlockSpec((k, tn), lambda i, j: (0, j + cb))],
        out_specs=[pl.BlockSpec((tm, tn), lambda i, j: (i, j)) for _ in out_dtypes],
        compiler_params=_cparams("parallel", "arbitrary"), name=name)(x, w)
    return outs


def _conv_kernel(za_ref, zb_ref, prev_ref, w_ref, cb_ref, lg_ref, lb_ref, o_ref, new_ref,
                 ext_ref, conv_ref, *, T, W):
    t = pl.program_id(1)

    @pl.when(t == 0)
    def _():
        ext_ref[0:CONV_HIST, :] = prev_ref[...]

    @pl.when(t > 0)
    def _():
        ext_ref[0:CONV_HIST, :] = ext_ref[T:T + CONV_HIST, :]

    ext_ref[CONV_HIST:CONV_HIST + T, :] = za_ref[...] * _sigmoid(zb_ref[...])
    rc = min(T, 32)
    base = CONV_HIST - (W - 1)
    for r0 in range(0, T, rc):
        acc = jnp.broadcast_to(cb_ref[...], (rc, cb_ref.shape[1]))
        for k in range(W):
            acc = acc + w_ref[k:k + 1, :] * ext_ref[base + k + r0:base + k + r0 + rc, :]
        conv_ref[r0:r0 + rc, :] = acc
    c = conv_ref[...]
    mu = jnp.mean(c, axis=-1, keepdims=True)
    xc = c - mu
    y = xc * lax.rsqrt(jnp.mean(xc * xc, axis=-1, keepdims=True) + EPS) * lg_ref[...] + lb_ref[...]
    o_ref[...] = (y * _sigmoid(y)).astype(o_ref.dtype)
    new_ref[...] = ext_ref[T:T + CONV_HIST, :]


def _conv_prompt(zglu, prev32, conv_w, conv_b, ln_g, ln_b, b, l):
    m, c2 = zglu.shape
    c = c2 // 2
    w = conv_w.shape[0]
    t = _tile(l, 256)
    nt = l // t
    kern = functools.partial(_conv_kernel, T=t, W=w)
    row = lambda bi, ti: (bi * nt + ti, 0)
    vec = pl.BlockSpec((1, c), lambda bi, ti: (0, 0))
    return pl.pallas_call(
        kern, out_shape=[SDS((m, c), bf16), SDS((b, CONV_HIST, c), f32)], grid=(b, nt),
        in_specs=[pl.BlockSpec((t, c), row), pl.BlockSpec((t, c), lambda bi, ti: (bi * nt + ti, 1)),
                  pl.BlockSpec((None, CONV_HIST, c), lambda bi, ti: (bi, 0, 0)),
                  pl.BlockSpec((w, c), lambda bi, ti: (0, 0)), vec, vec, vec],
        out_specs=[pl.BlockSpec((t, c), row), pl.BlockSpec((None, CONV_HIST, c), lambda bi, ti: (bi, 0, 0))],
        scratch_shapes=[pltpu.VMEM((t + CONV_HIST, c), f32), pltpu.VMEM((t, c), f32)],
        compiler_params=_cparams("parallel", "arbitrary"), name="conv_prompt",
    )(zglu, zglu, prev32, conv_w, conv_b.reshape(1, c), ln_g.reshape(1, c), ln_b.reshape(1, c))


def _conv_step_kernel(zg_ref, st_ref, w_ref, cb_ref, lg_ref, lb_ref, o_ref, new_ref, *, W):
    c = cb_ref.shape[1]
    nb = zg_ref.shape[0]
    a = zg_ref[:, 0:c] * _sigmoid(zg_ref[:, c:2 * c])
    rows = []
    for bi in range(nb):
        hist = st_ref[bi]
        conv = (jnp.sum(hist * w_ref[0:W - 1, :], axis=0, keepdims=True)
                + w_ref[W - 1:W, :] * a[bi:bi + 1, :] + cb_ref[...])
        rows.append(conv)
        new_ref[bi, 0:W - 2, :] = st_ref[bi, 1:W - 1, :]
        new_ref[bi, W - 2:W - 1, :] = a[bi:bi + 1, :]
    cv = jnp.concatenate(rows, axis=0)
    mu = jnp.mean(cv, axis=-1, keepdims=True)
    xc = cv - mu
    y = xc * lax.rsqrt(jnp.mean(xc * xc, axis=-1, keepdims=True) + EPS) * lg_ref[...] + lb_ref[...]
    o_ref[...] = (y * _sigmoid(y)).astype(o_ref.dtype)


def _conv_step(zglu, state, conv_w, conv_b, ln_g, ln_b):
    nb, c2 = zglu.shape
    c = c2 // 2
    w = conv_w.shape[0]
    return pl.pallas_call(
        functools.partial(_conv_step_kernel, W=w),
        out_shape=[SDS((nb, c), bf16), SDS((nb, w - 1, c), f32)],
        compiler_params=_cparams(), name="conv_step",
    )(zglu, state, conv_w, conv_b.reshape(1, c), ln_g.reshape(1, c), ln_b.reshape(1, c))


def _log_sigmoid(z):
    return jnp.minimum(z, 0.0) - jnp.log1p(jnp.exp(-jnp.abs(z)))


def _logf_kernel(zf_ref, b_ref, lf_ref, cum_ref, *, H):
    lf = _log_sigmoid(zf_ref[:, 0:H] + b_ref[...])
    lf_ref[...] = lf
    n = lf.shape[0]
    row = lax.broadcasted_iota(jnp.int32, lf.shape, 0)
    x = lf
    s = 1
    while s < n:
        x = x + jnp.where(row >= s, pltpu.roll(x, s, axis=0), 0.0)
        s *= 2
    cum_ref[...] = x


def _logf_prompt(zf, b_forget, b, l):
    h = b_forget.shape[0]
    return pl.pallas_call(
        functools.partial(_logf_kernel, H=h),
        out_shape=[SDS((b, l, h), f32), SDS((b, l, h), f32)], grid=(b,),
        in_specs=[pl.BlockSpec((l, zf.shape[1]), lambda i: (i, 0)), pl.BlockSpec((1, h), lambda i: (0, 0))],
        out_specs=[pl.BlockSpec((None, l, h), lambda i: (i, 0, 0))] * 2,
        compiler_params=_cparams("parallel"), name="logf_prompt")(zf, b_forget.reshape(1, h))


def _logf_step_kernel(zf_ref, b_ref, lf_ref, *, H):
    lf_ref[...] = _log_sigmoid(zf_ref[:, 0:H] + b_ref[...])


def _logf_step(zf, b_forget):
    h = b_forget.shape[0]
    return pl.pallas_call(functools.partial(_logf_step_kernel, H=h), out_shape=SDS((zf.shape[0], h), f32),
                          compiler_params=_cparams(), name="logf_step")(zf, b_forget.reshape(1, h))


def _flash_kernel(q_ref, k_ref, v_ref, cc_ref, cr_ref, o_ref, *, tq, scale, H):
    h = pl.program_id(1)
    n = q_ref.shape[0]
    nt = (((1,), (1,)), ((), ()))
    lane = lax.broadcasted_iota(jnp.int32, (n, H), 1)
    cq = jnp.sum(jnp.where(lane == h, cc_ref[...], 0.0), axis=1, keepdims=True)
    rr = lax.broadcasted_iota(jnp.int32, (tq, tq), 0)
    cc = lax.broadcasted_iota(jnp.int32, (tq, tq), 1)
    for i in range(n // tq):
        lo, hi = i * tq, (i + 1) * tq
        q = q_ref[lo:hi, :]
        cqi = cq[lo:hi, :]
        sd = lax.dot_general(q, k_ref[lo:hi, :], nt, preferred_element_type=f32) * scale
        sd = sd + cqi - cr_ref[:, lo:hi]
        sd = jnp.where(cc <= rr, sd, NEG_INF)
        m = jnp.max(sd, axis=1, keepdims=True)
        if i > 0:
            so = lax.dot_general(q, k_ref[0:lo, :], nt, preferred_element_type=f32) * scale
            so = so + cqi - cr_ref[:, 0:lo]
            m = jnp.maximum(m, jnp.max(so, axis=1, keepdims=True))
            po = jnp.exp(so - m)
            den = jnp.sum(po, axis=1, keepdims=True)
            acc = jnp.dot(po.astype(bf16), v_ref[0:lo, :], preferred_element_type=f32)
        pd = jnp.exp(sd - m)
        if i > 0:
            den = den + jnp.sum(pd, axis=1, keepdims=True)
            acc = acc + jnp.dot(pd.astype(bf16), v_ref[lo:hi, :], preferred_element_type=f32)
        else:
            den = jnp.sum(pd, axis=1, keepdims=True)
            acc = jnp.dot(pd.astype(bf16), v_ref[lo:hi, :], preferred_element_type=f32)
        o_ref[lo:hi, :] = (acc / den).astype(o_ref.dtype)


def _flash_prompt(q, k, v, cum, b, l, h, dh):
    m = q.shape[0]
    tq = _tile(l, 512)
    cum_row = jnp.transpose(cum, (0, 2, 1)).reshape(b, h, 1, l)
    blk = pl.BlockSpec((l, dh), lambda bi, hi: (bi, hi))
    return pl.pallas_call(
        functools.partial(_flash_kernel, tq=tq, scale=dh ** -0.5, H=h),
        out_shape=SDS((m, h * dh), bf16), grid=(b, h),
        in_specs=[blk, blk, blk,
                  pl.BlockSpec((None, l, h), lambda bi, hi: (bi, 0, 0)),
                  pl.BlockSpec((None, None, 1, l), lambda bi, hi: (bi, hi, 0, 0))],
        out_specs=blk,
        compiler_params=_cparams("parallel", "arbitrary"), name="flash_prompt")(q, k, v, cum, cum_row)


def _pagecum_kernel(lf_ref, cw_ref, tot_ref):
    pb, r, h = lf_ref.shape
    x = lf_ref[...].reshape(pb * r, h)
    rin = jnp.bitwise_and(lax.broadcasted_iota(jnp.int32, x.shape, 0), r - 1)
    s = 1
    while s < r:
        x = x + jnp.where(rin >= s, pltpu.roll(x, s, axis=0), 0.0)
        s *= 2
    x3 = x.reshape(pb, r, h)
    cw_ref[...] = x3
    tot_ref[...] = jnp.broadcast_to(x3[:, r - 1:r, :], x3.shape)


def _page_cumsum(cache_logf):
    d, n_pool, r, h = cache_logf.shape
    assert r & (r - 1) == 0
    pb = _tile(n_pool, 32)
    blk = pl.BlockSpec((None, pb, r, h), lambda di, pi: (di, pi, 0, 0))
    return pl.pallas_call(
        _pagecum_kernel, out_shape=[SDS(cache_logf.shape, f32)] * 2, grid=(d, n_pool // pb),
        in_specs=[blk], out_specs=[blk, blk],
        compiler_params=_cparams("parallel", "parallel"), name="page_cumsum")(cache_logf)


def _decode_kernel(pt_ref, q_ref, kn_ref, vn_ref, lfn_ref, k_ref, v_ref, cw_ref, tot_ref, o_ref,
                   m_s, l_s, acc_s, pref_s, *, scale, NP):
    del pt_ref
    p = pl.program_id(1)
    r, h, dh = k_ref.shape

    @pl.when(p == 0)
    def _():
        m_s[...] = jnp.full(m_s.shape, NEG_INF, f32)
        l_s[...] = jnp.zeros(l_s.shape, f32)
        acc_s[...] = jnp.zeros(acc_s.shape, f32)
        pref_s[...] = jnp.zeros(pref_s.shape, f32)

    q = q_ref[...]
    k2 = k_ref[...].reshape(r * h, dh).astype(bf16)
    v2 = v_ref[...].reshape(r * h, dh).astype(bf16)
    st = lax.dot_general(q.astype(bf16), k2, (((1,), (1,)), ((), ())), preferred_element_type=f32) * scale
    s = st - (pref_s[...] + cw_ref[...])
    lane = lax.broadcasted_iota(jnp.int32, s.shape, 1)
    sub = lax.broadcasted_iota(jnp.int32, s.shape, 0)
    s = jnp.where(jnp.bitwise_and(lane, h - 1) == sub, s, NEG_INF)
    m_old = m_s[...]
    m_new = jnp.maximum(m_old, jnp.max(s, axis=1, keepdims=True))
    alpha = jnp.exp(m_old - m_new)
    pr = jnp.exp(s - m_new)
    l_s[...] = alpha * l_s[...] + jnp.sum(pr, axis=1, keepdims=True)
    acc_s[...] = alpha * acc_s[...] + jnp.dot(pr.astype(bf16), v2, preferred_element_type=f32)
    m_s[...] = m_new
    pref_s[...] = pref_s[...] + tot_ref[...]

    @pl.when(p == NP - 1)
    def _():
        cn_row = pref_s[:, 0:h] + lfn_ref[...]
        eye = lax.broadcasted_iota(jnp.int32, (h, h), 0) == lax.broadcasted_iota(jnp.int32, (h, h), 1)
        cn = jnp.sum(jnp.where(eye, jnp.broadcast_to(cn_row, (h, h)), 0.0), axis=1, keepdims=True)
        s_new = jnp.sum(q * kn_ref[...], axis=1, keepdims=True) * scale - cn
        m_o = m_s[...]
        m_f = jnp.maximum(m_o, s_new)
        a = jnp.exp(m_o - m_f)
        pn = jnp.exp(s_new - m_f)
        den = a * l_s[...] + pn
        o_ref[...] = ((a * acc_s[...] + pn * vn_ref[...]) / den).astype(o_ref.dtype)


def _decode_attention(q, k_new, v_new, lf_new, cache_k, cache_v, cw_flat, tot_flat, page_table, layer):
    nb, h, dh = q.shape
    _, _, r, _, _ = cache_k.shape
    npg = page_table.shape[1]
    assert h & (h - 1) == 0 and h % SUBLANES == 0
    pt = page_table.reshape(-1).astype(jnp.int32)
    new = pl.BlockSpec((None, h, dh), lambda bi, pi, pt_ref: (bi, 0, 0))
    page = pl.BlockSpec((None, None, r, h, dh), lambda bi, pi, pt_ref: (layer, pt_ref[bi * npg + pi], 0, 0, 0))
    flat = pl.BlockSpec((None, None, 1, r * h), lambda bi, pi, pt_ref: (layer, pt_ref[bi * npg + pi], 0, 0))
    grid_spec = pltpu.PrefetchScalarGridSpec(
        num_scalar_prefetch=1, grid=(nb, npg),
        in_specs=[new, new, new, pl.BlockSpec((None, 1, h), lambda bi, pi, pt_ref: (bi, 0, 0)),
                  page, page, flat, flat],
        out_specs=new,
        scratch_shapes=[pltpu.VMEM((h, 1), f32), pltpu.VMEM((h, 1), f32), pltpu.VMEM((h, dh), f32),
                        pltpu.VMEM((1, r * h), f32)])
    return pl.pallas_call(
        functools.partial(_decode_kernel, scale=dh ** -0.5, NP=npg),
        out_shape=SDS((nb, h, dh), bf16), grid_spec=grid_spec,
        compiler_params=_cparams("parallel", "arbitrary"), name="decode_attention",
    )(pt, q, k_new, v_new, lf_new.reshape(nb, 1, h), cache_k, cache_v, cw_flat, tot_flat)


def _disc_kernel(lr_ref, li_ref, ldt_ref, br_ref, bi_ref, lbr_ref, lbi_ref, bbr_ref, bbi_ref, cr_s, ci_s):
    lr = lr_ref[...]
    li = li_ref[...]
    dt = jnp.exp(ldt_ref[...])
    er = jnp.exp(lr * dt)
    ang = li * dt
    lbr = er * jnp.cos(ang)
    lbi = er * jnp.sin(ang)
    lbr_ref[...] = lbr
    lbi_ref[...] = lbi
    nr = lbr - 1.0
    den = lr * lr + li * li
    cr_s[...] = (nr * lr + lbi * li) / den
    ci_s[...] = (lbi * lr - nr * li) / den
    for g in range(lr.shape[0]):
        cr = cr_s[g:g + 1, :]
        ci = ci_s[g:g + 1, :]
        br = br_ref[g]
        bi = bi_ref[g]
        bbr_ref[g] = cr * br - ci * bi
        bbi_ref[g] = cr * bi + ci * br


def _discretize(lam_re, lam_im, log_dt, b_re, b_im):
    d, g, n = lam_re.shape
    p = b_re.shape[-1]
    dg = d * g
    bt = lambda x: jnp.transpose(x, (0, 1, 3, 2)).reshape(dg, p, n)
    lbr, lbi, bbr, bbi = pl.pallas_call(
        _disc_kernel,
        out_shape=[SDS((dg, n), f32), SDS((dg, n), f32), SDS((dg, p, n), f32), SDS((dg, p, n), f32)],
        scratch_shapes=[pltpu.VMEM((dg, n), f32), pltpu.VMEM((dg, n), f32)],
        compiler_params=_cparams(), name="s5_discretize",
    )(lam_re.reshape(dg, n), lam_im.reshape(dg, n), log_dt.reshape(dg, 1), bt(b_re), bt(b_im))
    return (lbr.reshape(d, g, n), lbi.reshape(d, g, n), bbr.reshape(d, g, p, n), bbi.reshape(d, g, p, n))


def _s5_layout(lbr, lbi, bbr, bbi, c_re, c_im):
    g, n = lbr.shape
    p = bbr.shape[1]
    gb = LANES // p
    nj = g // gb
    lam = jnp.concatenate([lbr.reshape(nj, gb * n), lbi.reshape(nj, gb * n)], axis=1)
    eye = jnp.eye(gb, dtype=f32)

    def in_block(bb):
        return jnp.einsum('jgpn,gh->jgphn', bb.reshape(nj, gb, p, n), eye).reshape(nj, gb * p, gb * n)

    def out_block(c):
        return jnp.einsum('jgpn,gh->jhngp', c.reshape(nj, gb, p, n), eye).reshape(nj, gb * n, gb * p)

    wb = jnp.concatenate([in_block(bbr), in_block(bbi)], axis=2).astype(bf16)
    wc = jnp.concatenate([out_block(c_re), -out_block(c_im)], axis=1).astype(bf16)
    return lam, wb, wc


def _gelu_tanh(x):
    return 0.5 * x * (1.0 + jnp.tanh(math.sqrt(2.0 / math.pi) * (x + 0.044715 * (x * x * x))))


def _s5_tail(y, wg_ref, bg_ref):
    yg = _gelu_tanh(y)
    gate = jnp.dot(yg.astype(bf16), wg_ref[...].astype(bf16), preferred_element_type=f32) + bg_ref[...]
    return yg * _sigmoid(gate)


def _s5_kernel(u_ref, h0_ref, lam_ref, wb_ref, wc_ref, d_ref, wg_ref, bg_ref, y_ref, hfin_ref,
               hs_ref, st_ref, *, T, NJ, NC):
    @pl.when(pl.program_id(1) == 0)
    def _():
        st_ref[...] = h0_ref[...]

    u = u_ref[...]
    ub = u.astype(bf16)
    for j in range(NJ):
        r = jnp.dot(ub[:, LANES * j:LANES * (j + 1)], wb_ref[j], preferred_element_type=f32)
        for c in range(NC):
            hs_ref[c, pl.ds(j, T, stride=NJ), :] = r[:, LANES * c:LANES * (c + 1)]
    half = NC // 2
    lam_r = [lam_ref[:, LANES * c:LANES * (c + 1)] for c in range(half)]
    lam_i = [lam_ref[:, LANES * (half + c):LANES * (half + c + 1)] for c in range(half)]

    def body(t, h):
        r0 = pl.multiple_of(t * NJ, NJ)
        new = [None] * NC
        for c in range(half):
            hr, hi = h[c], h[half + c]
            nr = lam_r[c] * hr - lam_i[c] * hi + hs_ref[c, pl.ds(r0, NJ), :]
            ni = lam_r[c] * hi + lam_i[c] * hr + hs_ref[half + c, pl.ds(r0, NJ), :]
            hs_ref[c, pl.ds(r0, NJ), :] = nr
            hs_ref[half + c, pl.ds(r0, NJ), :] = ni
            new[c], new[half + c] = nr, ni
        return tuple(new)

    h = lax.fori_loop(0, T, body, tuple(st_ref[:, LANES * c:LANES * (c + 1)] for c in range(NC)),
                      unroll=8 if T % 8 == 0 else 1)
    for c in range(NC):
        st_ref[:, LANES * c:LANES * (c + 1)] = h[c]
    hfin_ref[...] = st_ref[...]
    ys = []
    for j in range(NJ):
        hj = jnp.concatenate([hs_ref[c, pl.ds(j, T, stride=NJ), :] for c in range(NC)], axis=1)
        ys.append(jnp.dot(hj.astype(bf16), wc_ref[j], preferred_element_type=f32))
    y = jnp.concatenate(ys, axis=1) + d_ref[...] * u
    y_ref[...] = _s5_tail(y, wg_ref, bg_ref).astype(y_ref.dtype)


def _s5_prompt(u, h0, lam, wb, wc, d_skip, w_glu, b_glu, b, l):
    m, c = u.shape
    nj, _, sw = wb.shape
    nc = sw // LANES
    assert nj == SUBLANES and c == nj * LANES
    t = _tile(l, 256)
    nt = l // t
    const2 = lambda shape: pl.BlockSpec(shape, lambda bi, ti: (0, 0))
    const3 = lambda shape: pl.BlockSpec(shape, lambda bi, ti: (0, 0, 0))
    y, hfin = pl.pallas_call(
        functools.partial(_s5_kernel, T=t, NJ=nj, NC=nc),
        out_shape=[SDS((m, c), bf16), SDS((b, nj, sw), f32)], grid=(b, nt),
        in_specs=[pl.BlockSpec((t, c), lambda bi, ti: (bi * nt + ti, 0)),
                  pl.BlockSpec((None, nj, sw), lambda bi, ti: (bi, 0, 0)),
                  const2((nj, sw)), const3((nj, LANES, sw)), const3((nj, sw, LANES)),
                  const2((1, c)), const2((c, c)), const2((1, c))],
        out_specs=[pl.BlockSpec((t, c), lambda bi, ti: (bi * nt + ti, 0)),
                   pl.BlockSpec((None, nj, sw), lambda bi, ti: (bi, 0, 0))],
        scratch_shapes=[pltpu.VMEM((nc, t * nj, LANES), f32), pltpu.VMEM((nj, sw), f32)],
        compiler_params=_cparams("parallel", "arbitrary"), name="s5_prompt",
    )(u, h0, lam, wb, wc, d_skip.reshape(1, c), w_glu, b_glu.reshape(1, c))
    return y, hfin


def _s5_step_kernel(u_ref, h0_ref, lam_ref, wb_ref, wc_ref, d_ref, wg_ref, bg_ref, y_ref, hn_ref, *, NJ):
    u = u_ref[...]
    ub = u.astype(bf16)
    half = lam_ref.shape[1] // 2
    ys = []
    for j in range(NJ):
        r = jnp.dot(ub[:, LANES * j:LANES * (j + 1)], wb_ref[j], preferred_element_type=f32)
        lr, li = lam_ref[j:j + 1, 0:half], lam_ref[j:j + 1, half:]
        hr, hi = h0_ref[j, :, 0:half], h0_ref[j, :, half:]
        nr = lr * hr - li * hi + r[:, 0:half]
        ni = lr * hi + li * hr + r[:, half:]
        hn_ref[j, :, 0:half] = nr
        hn_ref[j, :, half:] = ni
        hj = jnp.concatenate([nr, ni], axis=1).astype(bf16)
        ys.append(jnp.dot(hj, wc_ref[j], preferred_element_type=f32))
    y = jnp.concatenate(ys, axis=1) + d_ref[...] * u
    y_ref[...] = _s5_tail(y, wg_ref, bg_ref).astype(y_ref.dtype)


def _s5_step(u, h0, lam, wb, wc, d_skip, w_glu, b_glu):
    nb, c = u.shape
    nj, _, sw = wb.shape
    return pl.pallas_call(
        functools.partial(_s5_step_kernel, NJ=nj),
        out_shape=[SDS((nb, c), bf16), SDS((nj, nb, sw), f32)],
        compiler_params=_cparams(), name="s5_step",
    )(u, h0, lam, wb, wc, d_skip.reshape(1, c), w_glu, b_glu.reshape(1, c))


def _merge_kernel(a_ref, o_ref, y_ref, wa_ref, wo_ref, wy_ref, g0_ref, g1_ref, g2_ref, m_ref):
    ba = jnp.dot(a_ref[...], wa_ref[...].astype(bf16), preferred_element_type=f32)
    bb = jnp.dot(o_ref[...], wo_ref[...].astype(bf16), preferred_element_type=f32)
    bc = jnp.dot(y_ref[...], wy_ref[...].astype(bf16), preferred_element_type=f32)
    mix = _sigmoid(g0_ref[...]) * ba + _sigmoid(g1_ref[...]) * bb + _sigmoid(g2_ref[...]) * bc
    m_ref[...] = mix.astype(m_ref.dtype)


def _merge(a, o, y, w_conv, w_attn, w_ssm, gates):
    m = a.shape[0]
    d = w_conv.shape[1]
    tm = _tile(m, 512)
    tn = _tile(d, 512)
    nb = d // tn
    lhs = lambda x: pl.BlockSpec((tm, x.shape[1]), lambda i, j: (i, 0))
    wsp = lambda w: pl.BlockSpec((w.shape[0], tn), lambda i, j: (0, j))
    gate = lambda gi: pl.BlockSpec((tm, tn), lambda i, j: (i, j + gi * nb))
    return pl.pallas_call(
        _merge_kernel, out_shape=SDS((m, d), bf16), grid=(m // tm, nb),
        in_specs=[lhs(a), lhs(o), lhs(y), wsp(w_conv), wsp(w_attn), wsp(w_ssm), gate(0), gate(1), gate(2)],
        out_specs=pl.BlockSpec((tm, tn), lambda i, j: (i, j)),
        compiler_params=_cparams("parallel", "arbitrary"), name="branch_merge",
    )(a, o, y, w_conv, w_attn, w_ssm, gates, gates, gates)


def _resnorm_kernel(x_ref, t_ref, g_ref, g2_ref, xo_ref, h_ref):
    xn = x_ref[...] + _rms(t_ref[...], g_ref[...])
    xo_ref[...] = xn
    h_ref[...] = _rms(xn, g2_ref[...]).astype(h_ref.dtype)


def _resnorm_last_kernel(x_ref, t_ref, g_ref, xo_ref):
    xo_ref[...] = x_ref[...] + _rms(t_ref[...], g_ref[...])


def _resnorm(x, t, g, g_next=None):
    m, d = x.shape
    tm = _tile(m, 256)
    row = pl.BlockSpec((tm, d), lambda i: (i, 0))
    vec = pl.BlockSpec((1, d), lambda i: (0, 0))
    if g_next is None:
        return pl.pallas_call(
            _resnorm_last_kernel, out_shape=SDS((m, d), f32), grid=(m // tm,),
            in_specs=[row, row, vec], out_specs=row,
            compiler_params=_cparams("parallel"), name="resnorm_last")(x, t, g.reshape(1, d)), None
    return pl.pallas_call(
        _resnorm_kernel, out_shape=[SDS((m, d), f32), SDS((m, d), bf16)], grid=(m // tm,),
        in_specs=[row, row, vec, vec], out_specs=[row, row],
        compiler_params=_cparams("parallel"), name="resnorm")(x, t, g.reshape(1, d), g_next.reshape(1, d))


def _ffn_gu_kernel(h_ref, wg_ref, wu_ref, o_ref):
    h = h_ref[...]
    g = jnp.dot(h, wg_ref[...].astype(bf16), preferred_element_type=f32)
    u = jnp.dot(h, wu_ref[...].astype(bf16), preferred_element_type=f32)
    o_ref[...] = (g * _sigmoid(g) * u).astype(o_ref.dtype)


def _ffn_gate_up(h, w_gate, w_up):
    m, d = h.shape
    f = w_gate.shape[1]
    tm = _tile(m, 1024)
    tn = _tile(f, 256)
    wsp = pl.BlockSpec((d, tn), lambda i, j: (0, j))
    return pl.pallas_call(
        _ffn_gu_kernel, out_shape=SDS((m, f), bf16), grid=(m // tm, f // tn),
        in_specs=[pl.BlockSpec((tm, d), lambda i, j: (i, 0)), wsp, wsp],
        out_specs=pl.BlockSpec((tm, tn), lambda i, j: (i, j)),
        compiler_params=_cparams("parallel", "arbitrary"), name="ffn_gate_up")(h, w_gate, w_up)


def _run_layer(x, b, l, p, s5p, conv_prev, h0, past):
    d = x.shape[1]
    heads = p['b_forget'].shape[0]
    c_conv = p['conv_b'].shape[0]
    d_attn = p['w_branch_attn'].shape[0]
    dh = d_attn // heads
    c_ssm = p['w_glu'].shape[0]
    off_q = 2 * c_conv
    off_k = off_q + d_attn
    off_v = off_k + d_attn
    off_f = off_v + d_attn
    off_u = off_f + heads
    off_g = off_u + c_ssm
    w_in = p['w_in']
    w_f = jnp.pad(w_in[:, off_f:off_u], ((0, 0), (0, LANES - heads)))
    w_u = w_in[:, off_u:off_g]
    w_g = w_in[:, off_g:]

    h = _rmsnorm(x, p['g_mix_pre'])
    zglu, = _mm(h, w_in, [f32], 0, off_q, name="in_glu")
    q, = _mm(h, w_in, [f32 if past is not None else bf16], off_q, d_attn, name="in_q")
    k, kb = _mm(h, w_in, [f32, bf16], off_k, d_attn, name="in_k")
    v, vb = _mm(h, w_in, [f32, bf16], off_v, d_attn, name="in_v")
    zf, = _mm(h, w_f, [f32], name="in_f")
    u, = _mm(h, w_u, [f32], name="in_u")
    gates, = _mm(h, w_g, [f32], name="in_gates")

    lam, wb, wc = s5p
    if past is None:
        a, conv_new = _conv_prompt(zglu, conv_prev, p['conv_w'], p['conv_b'], p['ln_conv_g'], p['ln_conv_b'], b, l)
        conv_new = conv_new[:, CONV_HIST - (p['conv_w'].shape[0] - 1):, :]
        logf, cum = _logf_prompt(zf, p['b_forget'], b, l)
        o = _flash_prompt(q, kb, vb, cum, b, l, heads, dh)
        y, hfin = _s5_prompt(u, h0, lam, wb, wc, p['ssm_d'].reshape(-1), p['w_glu'], p['b_glu'], b, l)
        logf = logf.reshape(b, l, heads)
    else:
        a, conv_new = _conv_step(zglu, conv_prev, p['conv_w'], p['conv_b'], p['ln_conv_g'], p['ln_conv_b'])
        logf = _logf_step(zf, p['b_forget'])
        cache_k, cache_v, cw_flat, tot_flat, page_table, layer = past
        o = _decode_attention(q.reshape(b, heads, dh), k.reshape(b, heads, dh), v.reshape(b, heads, dh), logf,
                              cache_k, cache_v, cw_flat, tot_flat, page_table, layer).reshape(b, d_attn)
        y, hn = _s5_step(u, jnp.transpose(h0, (1, 0, 2)), lam, wb, wc, p['ssm_d'].reshape(-1),
                         p['w_glu'], p['b_glu'])
        hfin = jnp.transpose(hn, (1, 0, 2))
        logf = logf.reshape(b, l, heads)

    mix = _merge(a, o, y, p['w_branch_conv'], p['w_branch_attn'], p['w_branch_ssm'], gates)
    t, = _mm(mix, p['w_out'], [f32], name="w_out")
    x1, hf = _resnorm(x, t, p['g_mix_post'], p['g_ffn_pre'])
    act = _ffn_gate_up(hf, p['w_ffn_gate'], p['w_ffn_up'])
    f, = _mm(act, p['w_ffn_down'], [f32], tm=512, tn=256, name="ffn_down")
    x2, _ = _resnorm(x1, f, p['g_ffn_post'])

    n_groups, n_state = p['ssm_lambda_re'].shape
    half = hfin.shape[2] // 2
    h_re = hfin[:, :, :half].reshape(b, n_groups, n_state)
    h_im = hfin[:, :, half:].reshape(b, n_groups, n_state)
    return (x2, k.reshape(b, l, heads, dh), v.reshape(b, l, heads, dh), logf, conv_new, h_re, h_im)


def _pack_state(h_re, h_im, nj):
    b = h_re.shape[0]
    return jnp.concatenate([h_re.reshape(b, nj, -1), h_im.reshape(b, nj, -1)], axis=2)


def kernel(x_prompt, x_sample, cache_k, cache_v, cache_logf, state_conv, state_ssm_re, state_ssm_im, page_table, g_mix_pre, w_in, b_forget, conv_w, conv_b, ln_conv_g, ln_conv_b, ssm_lambda_re, ssm_lambda_im, ssm_log_dt, ssm_b_re, ssm_b_im, ssm_c_re, ssm_c_im, ssm_d, w_glu, b_glu, w_branch_conv, w_branch_attn, w_branch_ssm, w_out, g_mix_post, g_ffn_pre, w_ffn_gate, w_ffn_up, w_ffn_down, g_ffn_post):
    params = dict(g_mix_pre=g_mix_pre, w_in=w_in, b_forget=b_forget, conv_w=conv_w, conv_b=conv_b,
                  ln_conv_g=ln_conv_g, ln_conv_b=ln_conv_b, ssm_lambda_re=ssm_lambda_re, ssm_d=ssm_d,
                  w_glu=w_glu, b_glu=b_glu, w_branch_conv=w_branch_conv, w_branch_attn=w_branch_attn,
                  w_branch_ssm=w_branch_ssm, w_out=w_out, g_mix_post=g_mix_post, g_ffn_pre=g_ffn_pre,
                  w_ffn_gate=w_ffn_gate, w_ffn_up=w_ffn_up, w_ffn_down=w_ffn_down, g_ffn_post=g_ffn_post)
    depth = w_in.shape[0]
    bp, lp, d = x_prompt.shape
    bs, ls, _ = x_sample.shape
    assert ls == 1
    n_groups, n_state = ssm_lambda_re.shape[1:]
    p_ssm = ssm_b_re.shape[-1]
    nj = n_groups * p_ssm // LANES
    c_conv = conv_b.shape[1]
    width = conv_w.shape[1]

    lbr, lbi, bbr, bbi = _discretize(ssm_lambda_re, ssm_lambda_im, ssm_log_dt, ssm_b_re, ssm_b_im)
    cw, tot = _page_cumsum(cache_logf)
    n_pool, rows, heads = cache_logf.shape[1:]
    cw_flat = cw.reshape(depth, n_pool, 1, rows * heads)
    tot_flat = tot.reshape(depth, n_pool, 1, rows * heads)

    y_p = x_prompt.reshape(bp * lp, d)
    y_s = x_sample.reshape(bs * ls, d)
    outs_p, outs_s = [], []
    for layer in range(depth):
        p = {name: val[layer] for name, val in params.items()}
        s5p = _s5_layout(lbr[layer], lbi[layer], bbr[layer], bbi[layer], ssm_c_re[layer], ssm_c_im[layer])
        conv0 = jnp.zeros((bp, CONV_HIST, c_conv), f32)
        h0 = jnp.zeros((bp, nj, 2 * n_groups * n_state // nj), f32)
        res = _run_layer(y_p, bp, lp, p, s5p, conv0, h0, None)
        y_p = res[0]
        outs_p.append(res[1:])
        past = (cache_k, cache_v, cw_flat, tot_flat, page_table, layer)
        res = _run_layer(y_s, bs, ls, p, s5p, state_conv[layer],
                         _pack_state(state_ssm_re[layer], state_ssm_im[layer], nj), past)
        y_s = res[0]
        outs_s.append(res[1:])
    stack = lambda outs, i: jnp.stack([o[i] for o in outs])
    return ((y_p.reshape(bp, lp, d), y_s.reshape(bs, ls, d))
            + tuple(stack(outs_p, i) for i in range(6))
            + tuple(stack(outs_s, i) for i in range(6)))
```

```python
import functools
import math

import jax
import jax.numpy as jnp
from jax import lax
from jax.experimental import pallas as pl
from jax.experimental.pallas import tpu as pltpu

f32 = jnp.float32
bf16 = jnp.bfloat16
SDS = jax.ShapeDtypeStruct

LANES = 128
SUBLANES = 8
VMEM_LIMIT_BYTES = 56 * 2 ** 20
EPS = 1e-6
NEG_INF = -1e30
CONV_HIST = 32


def _cparams(*sem):
    return pltpu.CompilerParams(dimension_semantics=sem, vmem_limit_bytes=VMEM_LIMIT_BYTES)


def _sigmoid(x):
    return 1.0 / (1.0 + jnp.exp(-x))


def _tile(n, pref):
    if n <= pref:
        return n
    t = pref
    while n % t:
        t //= 2
    return t


def _rms(x, g):
    return x * lax.rsqrt(jnp.mean(x * x, axis=-1, keepdims=True) + EPS) * g


def _rmsnorm_kernel(x_ref, g_ref, o_ref):
    o_ref[...] = _rms(x_ref[...], g_ref[...]).astype(o_ref.dtype)


def _rmsnorm(x, g):
    m, d = x.shape
    tm = _tile(m, 256)
    return pl.pallas_call(
        _rmsnorm_kernel, out_shape=SDS((m, d), bf16), grid=(m // tm,),
        in_specs=[pl.BlockSpec((tm, d), lambda i: (i, 0)), pl.BlockSpec((1, d), lambda i: (0, 0))],
        out_specs=pl.BlockSpec((tm, d), lambda i: (i, 0)),
        compiler_params=_cparams("parallel"), name="rmsnorm")(x, g.reshape(1, d))


NT_DIMS = (((1,), (1,)), ((), ()))
ROW_ALIGN = 16


def _mm_kernel(x_ref, w_ref, *rest, n_out, nt, staged):
    o_refs = rest[:n_out]
    if staged:
        wb_ref = rest[n_out]

        @pl.when(pl.program_id(1) == 0)
        def _():
            wb_ref[...] = (w_ref[0] if nt else w_ref[...]).astype(bf16)

        w = wb_ref[...]
    else:
        w = (w_ref[0] if nt else w_ref[...]).astype(bf16)
    if nt:
        acc = lax.dot_general(x_ref[...], w, NT_DIMS, preferred_element_type=f32)
    else:
        acc = jnp.dot(x_ref[...], w, preferred_element_type=f32)
    for o in o_refs:
        o[...] = acc.astype(o.dtype)


def _mm(x, w, layer, out_dtypes, *, nt=False, off=0, n=None, tm=1024, tn=512, weights_outer=True, name="mm"):
    m, k = x.shape
    n_total = w.shape[1] if nt else w.shape[2]
    n = n_total - off if n is None else n
    tm = _tile(m, tm)
    tn = _tile(n, tn)
    assert n % tn == 0
    if weights_outer:
        grid = (n // tn, m // tm)
        jj = lambda a, b: a
        ii = lambda a, b: b
        xspec = pl.BlockSpec((tm, k), lambda a, b: (b, 0))
    else:
        grid = (m // tm, n // tn)
        jj = lambda a, b: b
        ii = lambda a, b: a
        xspec = pl.BlockSpec((tm, k), lambda a, b: (a, 0), pipeline_mode=pl.Buffered(1))
    if nt:
        assert off % ROW_ALIGN == 0 and tn % ROW_ALIGN == 0
        wspec = pl.BlockSpec((pl.Element(1), pl.Element(tn), pl.Element(k)),
                             lambda a, b: (layer, pl.multiple_of(off + jj(a, b) * tn, ROW_ALIGN), 0))
        wshape = (tn, k)
    else:
        assert off % tn == 0
        wspec = pl.BlockSpec((None, k, tn), lambda a, b: (layer, 0, jj(a, b) + off // tn))
        wshape = (k, tn)
    staged = weights_outer and w.dtype != bf16
    return pl.pallas_call(
        functools.partial(_mm_kernel, n_out=len(out_dtypes), nt=nt, staged=staged),
        out_shape=[SDS((m, n), dt) for dt in out_dtypes], grid=grid,
        in_specs=[xspec, wspec],
        out_specs=[pl.BlockSpec((tm, tn), lambda a, b: (ii(a, b), jj(a, b))) for _ in out_dtypes],
        scratch_shapes=[pltpu.VMEM(wshape, bf16)] if staged else [],
        compiler_params=_cparams("arbitrary", "arbitrary"), name=name)(x, w)


def _conv_kernel(za_ref, zb_ref, prev_ref, w_ref, cb_ref, lg_ref, lb_ref, o_ref, new_ref,
                 ext_ref, conv_ref, *, T, W):
    t = pl.program_id(1)

    @pl.when(t == 0)
    def _():
        ext_ref[0:CONV_HIST, :] = prev_ref[...]

    @pl.when(t > 0)
    def _():
        ext_ref[0:CONV_HIST, :] = ext_ref[T:T + CONV_HIST, :]

    ext_ref[CONV_HIST:CONV_HIST + T, :] = za_ref[...] * _sigmoid(zb_ref[...])
    rc = min(T, 32)
    base = CONV_HIST - (W - 1)
    for r0 in range(0, T, rc):
        acc = jnp.broadcast_to(cb_ref[...], (rc, cb_ref.shape[1]))
        for k in range(W):
            acc = acc + w_ref[k:k + 1, :] * ext_ref[base + k + r0:base + k + r0 + rc, :]
        conv_ref[r0:r0 + rc, :] = acc
    c = conv_ref[...]
    mu = jnp.mean(c, axis=-1, keepdims=True)
    xc = c - mu
    y = xc * lax.rsqrt(jnp.mean(xc * xc, axis=-1, keepdims=True) + EPS) * lg_ref[...] + lb_ref[...]
    o_ref[...] = (y * _sigmoid(y)).astype(o_ref.dtype)
    new_ref[...] = ext_ref[T:T + CONV_HIST, :]


def _conv_prompt(zglu, prev32, conv_w, conv_b, ln_g, ln_b, b, l):
    m, c2 = zglu.shape
    c = c2 // 2
    w = conv_w.shape[0]
    t = _tile(l, 256)
    nt = l // t
    kern = functools.partial(_conv_kernel, T=t, W=w)
    row = lambda bi, ti: (bi * nt + ti, 0)
    vec = pl.BlockSpec((1, c), lambda bi, ti: (0, 0))
    return pl.pallas_call(
        kern, out_shape=[SDS((m, c), bf16), SDS((b, CONV_HIST, c), f32)], grid=(b, nt),
        in_specs=[pl.BlockSpec((t, c), row), pl.BlockSpec((t, c), lambda bi, ti: (bi * nt + ti, 1)),
                  pl.BlockSpec((None, CONV_HIST, c), lambda bi, ti: (bi, 0, 0)),
                  pl.BlockSpec((w, c), lambda bi, ti: (0, 0)), vec, vec, vec],
        out_specs=[pl.BlockSpec((t, c), row), pl.BlockSpec((None, CONV_HIST, c), lambda bi, ti: (bi, 0, 0))],
        scratch_shapes=[pltpu.VMEM((t + CONV_HIST, c), f32), pltpu.VMEM((t, c), f32)],
        compiler_params=_cparams("parallel", "arbitrary"), name="conv_prompt",
    )(zglu, zglu, prev32, conv_w, conv_b.reshape(1, c), ln_g.reshape(1, c), ln_b.reshape(1, c))


def _conv_step_kernel(zg_ref, st_ref, w_ref, cb_ref, lg_ref, lb_ref, o_ref, new_ref, *, W):
    c = cb_ref.shape[1]
    nb = zg_ref.shape[0]
    a = zg_ref[:, 0:c] * _sigmoid(zg_ref[:, c:2 * c])
    rows = []
    for bi in range(nb):
        hist = st_ref[bi]
        conv = (jnp.sum(hist * w_ref[0:W - 1, :], axis=0, keepdims=True)
                + w_ref[W - 1:W, :] * a[bi:bi + 1, :] + cb_ref[...])
        rows.append(conv)
        new_ref[bi, 0:W - 2, :] = st_ref[bi, 1:W - 1, :]
        new_ref[bi, W - 2:W - 1, :] = a[bi:bi + 1, :]
    cv = jnp.concatenate(rows, axis=0)
    mu = jnp.mean(cv, axis=-1, keepdims=True)
    xc = cv - mu
    y = xc * lax.rsqrt(jnp.mean(xc * xc, axis=-1, keepdims=True) + EPS) * lg_ref[...] + lb_ref[...]
    o_ref[...] = (y * _sigmoid(y)).astype(o_ref.dtype)


def _conv_step(zglu, state, conv_w, conv_b, ln_g, ln_b):
    nb, c2 = zglu.shape
    c = c2 // 2
    w = conv_w.shape[0]
    return pl.pallas_call(
        functools.partial(_conv_step_kernel, W=w),
        out_shape=[SDS((nb, c), bf16), SDS((nb, w - 1, c), f32)],
        compiler_params=_cparams(), name="conv_step",
    )(zglu, state, conv_w, conv_b.reshape(1, c), ln_g.reshape(1, c), ln_b.reshape(1, c))


def _log_sigmoid(z):
    return jnp.minimum(z, 0.0) - jnp.log1p(jnp.exp(-jnp.abs(z)))


def _logf_kernel(zf_ref, b_ref, lf_ref, cum_ref, *, H):
    lf = _log_sigmoid(zf_ref[:, 0:H] + b_ref[...])
    lf_ref[...] = lf
    n = lf.shape[0]
    row = lax.broadcasted_iota(jnp.int32, lf.shape, 0)
    x = lf
    s = 1
    while s < n:
        x = x + jnp.where(row >= s, pltpu.roll(x, s, axis=0), 0.0)
        s *= 2
    cum_ref[...] = x


def _logf_prompt(zf, b_forget, b, l):
    h = b_forget.shape[0]
    return pl.pallas_call(
        functools.partial(_logf_kernel, H=h),
        out_shape=[SDS((b, l, h), f32), SDS((b, l, h), f32)], grid=(b,),
        in_specs=[pl.BlockSpec((l, zf.shape[1]), lambda i: (i, 0)), pl.BlockSpec((1, h), lambda i: (0, 0))],
        out_specs=[pl.BlockSpec((None, l, h), lambda i: (i, 0, 0))] * 2,
        compiler_params=_cparams("parallel"), name="logf_prompt")(zf, b_forget.reshape(1, h))


def _logf_step_kernel(zf_ref, b_ref, lf_ref, *, H):
    lf_ref[...] = _log_sigmoid(zf_ref[:, 0:H] + b_ref[...])


def _logf_step(zf, b_forget):
    h = b_forget.shape[0]
    return pl.pallas_call(functools.partial(_logf_step_kernel, H=h), out_shape=SDS((zf.shape[0], h), f32),
                          compiler_params=_cparams(), name="logf_step")(zf, b_forget.reshape(1, h))


def _flash_kernel(q_ref, k_ref, v_ref, cc_ref, cr_ref, o_ref, *, tq, scale, H):
    h = pl.program_id(1)
    n = q_ref.shape[0]
    nt = (((1,), (1,)), ((), ()))
    lane = lax.broadcasted_iota(jnp.int32, (n, H), 1)
    cq = jnp.sum(jnp.where(lane == h, cc_ref[...], 0.0), axis=1, keepdims=True)
    rr = lax.broadcasted_iota(jnp.int32, (tq, tq), 0)
    cc = lax.broadcasted_iota(jnp.int32, (tq, tq), 1)
    for i in range(n // tq):
        lo, hi = i * tq, (i + 1) * tq
        q = q_ref[lo:hi, :]
        cqi = cq[lo:hi, :]
        sd = lax.dot_general(q, k_ref[lo:hi, :], nt, preferred_element_type=f32) * scale
        sd = sd + cqi - cr_ref[:, lo:hi]
        sd = jnp.where(cc <= rr, sd, NEG_INF)
        m = jnp.max(sd, axis=1, keepdims=True)
        if i > 0:
            so = lax.dot_general(q, k_ref[0:lo, :], nt, preferred_element_type=f32) * scale
            so = so + cqi - cr_ref[:, 0:lo]
            m = jnp.maximum(m, jnp.max(so, axis=1, keepdims=True))
            po = jnp.exp(so - m)
            den = jnp.sum(po, axis=1, keepdims=True)
            acc = jnp.dot(po.astype(bf16), v_ref[0:lo, :], preferred_element_type=f32)
        pd = jnp.exp(sd - m)
        if i > 0:
            den = den + jnp.sum(pd, axis=1, keepdims=True)
            acc = acc + jnp.dot(pd.astype(bf16), v_ref[lo:hi, :], preferred_element_type=f32)
        else:
            den = jnp.sum(pd, axis=1, keepdims=True)
            acc = jnp.dot(pd.astype(bf16), v_ref[lo:hi, :], preferred_element_type=f32)
        o_ref[lo:hi, :] = (acc / den).astype(o_ref.dtype)


def _flash_prompt(q, k, v, cum, b, l, h, dh):
    m = q.shape[0]
    tq = _tile(l, 512)
    cum_row = jnp.transpose(cum, (0, 2, 1)).reshape(b, h, 1, l)
    blk = pl.BlockSpec((l, dh), lambda bi, hi: (bi, hi))
    return pl.pallas_call(
        functools.partial(_flash_kernel, tq=tq, scale=dh ** -0.5, H=h),
        out_shape=SDS((m, h * dh), bf16), grid=(b, h),
        in_specs=[blk, blk, blk,
                  pl.BlockSpec((None, l, h), lambda bi, hi: (bi, 0, 0)),
                  pl.BlockSpec((None, None, 1, l), lambda bi, hi: (bi, hi, 0, 0))],
        out_specs=blk,
        compiler_params=_cparams("parallel", "arbitrary"), name="flash_prompt")(q, k, v, cum, cum_row)


PAGE_GROUP = SUBLANES


def _decode_kernel(pt_ref, q_ref, kn_ref, vn_ref, lfn_ref, *rest, scale, NPG, G):
    k_refs, v_refs, f_refs = rest[0:G], rest[G:2 * G], rest[2 * G:3 * G]
    o_ref = rest[3 * G]
    m_s, l_s, acc_s, pref_s = rest[3 * G + 1:]
    bi = pl.program_id(0)
    p = pl.program_id(1)
    r, h, dh = k_refs[0].shape
    n = r * h

    @pl.when(p == 0)
    def _():
        m_s[...] = jnp.full(m_s.shape, NEG_INF, f32)
        l_s[...] = jnp.zeros(l_s.shape, f32)
        acc_s[...] = jnp.zeros(acc_s.shape, f32)
        pref_s[...] = jnp.zeros(pref_s.shape, f32)

    rows = []
    for g in range(G):
        pg = pt_ref[bi * NPG + p * G + g]
        rows.append(f_refs[g][pl.ds(jnp.bitwise_and(pg, PAGE_GROUP - 1), 1), :])
    lf = jnp.concatenate(rows, axis=0) if G > 1 else rows[0]
    lane = lax.broadcasted_iota(jnp.int32, lf.shape, 1)
    cw, tot = lf, lf
    s = h
    while s < n:
        cw = cw + jnp.where(lane >= s, pltpu.roll(cw, s, axis=1), 0.0)
        tot = tot + pltpu.roll(tot, s, axis=1)
        s *= 2
    off = pref_s[...]
    biases = []
    for g in range(G):
        biases.append(off + cw[g:g + 1, :])
        off = off + tot[g:g + 1, :]
    pref_s[...] = off
    bias = jnp.concatenate(biases, axis=1) if G > 1 else biases[0]

    q = q_ref[...]
    qb = q.astype(bf16)
    sts = [lax.dot_general(qb, k_refs[g][...].reshape(n, dh).astype(bf16), NT_DIMS,
                           preferred_element_type=f32) for g in range(G)]
    st = (jnp.concatenate(sts, axis=1) if G > 1 else sts[0]) * scale
    s = st - bias
    lane = lax.broadcasted_iota(jnp.int32, s.shape, 1)
    sub = lax.broadcasted_iota(jnp.int32, s.shape, 0)
    s = jnp.where(jnp.bitwise_and(lane, h - 1) == sub, s, NEG_INF)
    m_old = m_s[...]
    m_new = jnp.maximum(m_old, jnp.max(s, axis=1, keepdims=True))
    alpha = jnp.exp(m_old - m_new)
    pr = jnp.exp(s - m_new)
    l_s[...] = alpha * l_s[...] + jnp.sum(pr, axis=1, keepdims=True)
    acc = alpha * acc_s[...]
    for g in range(G):
        acc = acc + jnp.dot(pr[:, g * n:(g + 1) * n].astype(bf16), v_refs[g][...].reshape(n, dh).astype(bf16),
                            preferred_element_type=f32)
    acc_s[...] = acc
    m_s[...] = m_new

    @pl.when(p == NPG // G - 1)
    def _():
        cn_row = pref_s[:, 0:h] + lfn_ref[...]
        eye = lax.broadcasted_iota(jnp.int32, (h, h), 0) == lax.broadcasted_iota(jnp.int32, (h, h), 1)
        cn = jnp.sum(jnp.where(eye, jnp.broadcast_to(cn_row, (h, h)), 0.0), axis=1, keepdims=True)
        s_new = jnp.sum(q * kn_ref[...], axis=1, keepdims=True) * scale - cn
        m_o = m_s[...]
        m_f = jnp.maximum(m_o, s_new)
        a = jnp.exp(m_o - m_f)
        pn = jnp.exp(s_new - m_f)
        den = a * l_s[...] + pn
        o_ref[...] = ((a * acc_s[...] + pn * vn_ref[...]) / den).astype(o_ref.dtype)


def _flatten_logf(cache_logf):
    d, n_pool, r, h = cache_logf.shape
    pad = -n_pool % PAGE_GROUP
    flat = cache_logf.reshape(d, n_pool, r * h)
    if pad:
        flat = jnp.pad(flat, ((0, 0), (0, pad), (0, 0)))
    return flat.reshape(d, (n_pool + pad) // PAGE_GROUP, PAGE_GROUP, r * h)


def _decode_attention(q, k_new, v_new, lf_new, cache_k, cache_v, logf_flat, page_table, layer):
    nb, h, dh = q.shape
    _, _, r, _, _ = cache_k.shape
    npg = page_table.shape[1]
    assert h & (h - 1) == 0 and h % SUBLANES == 0 and r & (r - 1) == 0
    g = 4 if npg % 4 == 0 else (2 if npg % 2 == 0 else 1)
    pt = page_table.reshape(-1).astype(jnp.int32)
    page_id = lambda bi, pi, pt_ref, gi: pt_ref[bi * npg + pi * g + gi]
    new = pl.BlockSpec((None, h, dh), lambda bi, pi, pt_ref: (bi, 0, 0))
    page = [pl.BlockSpec((None, None, r, h, dh),
                         functools.partial(lambda bi, pi, pt_ref, gi: (layer, page_id(bi, pi, pt_ref, gi), 0, 0, 0), gi=gi))
            for gi in range(g)]
    flat = [pl.BlockSpec((None, None, PAGE_GROUP, r * h),
                         functools.partial(lambda bi, pi, pt_ref, gi:
                                           (layer, page_id(bi, pi, pt_ref, gi) // PAGE_GROUP, 0, 0), gi=gi))
            for gi in range(g)]
    grid_spec = pltpu.PrefetchScalarGridSpec(
        num_scalar_prefetch=1, grid=(nb, npg // g),
        in_specs=[new, new, new, pl.BlockSpec((None, 1, h), lambda bi, pi, pt_ref: (bi, 0, 0))]
        + page + page + flat,
        out_specs=new,
        scratch_shapes=[pltpu.VMEM((h, 1), f32), pltpu.VMEM((h, 1), f32), pltpu.VMEM((h, dh), f32),
                        pltpu.VMEM((1, r * h), f32)])
    return pl.pallas_call(
        functools.partial(_decode_kernel, scale=dh ** -0.5, NPG=npg, G=g),
        out_shape=SDS((nb, h, dh), bf16), grid_spec=grid_spec,
        compiler_params=_cparams("parallel", "arbitrary"), name="decode_attention",
    )(pt, q, k_new, v_new, lf_new.reshape(nb, 1, h), *([cache_k] * g), *([cache_v] * g), *([logf_flat] * g))


def _disc_kernel(lr_ref, li_ref, ldt_ref, br_ref, bi_ref, lbr_ref, lbi_ref, bbr_ref, bbi_ref, cr_s, ci_s):
    lr = lr_ref[...]
    li = li_ref[...]
    dt = jnp.exp(ldt_ref[...])
    er = jnp.exp(lr * dt)
    ang = li * dt
    lbr = er * jnp.cos(ang)
    lbi = er * jnp.sin(ang)
    lbr_ref[...] = lbr
    lbi_ref[...] = lbi
    nr = lbr - 1.0
    den = lr * lr + li * li
    cr_s[...] = (nr * lr + lbi * li) / den
    ci_s[...] = (lbi * lr - nr * li) / den
    for g in range(lr.shape[0]):
        cr = cr_s[g:g + 1, :]
        ci = ci_s[g:g + 1, :]
        br = br_ref[g]
        bi = bi_ref[g]
        bbr_ref[g] = cr * br - ci * bi
        bbi_ref[g] = cr * bi + ci * br


def _discretize(lam_re, lam_im, log_dt, b_re, b_im):
    d, g, n = lam_re.shape
    p = b_re.shape[-1]
    dg = d * g
    bt = lambda x: jnp.transpose(x, (0, 1, 3, 2)).reshape(dg, p, n)
    lbr, lbi, bbr, bbi = pl.pallas_call(
        _disc_kernel,
        out_shape=[SDS((dg, n), f32), SDS((dg, n), f32), SDS((dg, p, n), f32), SDS((dg, p, n), f32)],
        scratch_shapes=[pltpu.VMEM((dg, n), f32), pltpu.VMEM((dg, n), f32)],
        compiler_params=_cparams(), name="s5_discretize",
    )(lam_re.reshape(dg, n), lam_im.reshape(dg, n), log_dt.reshape(dg, 1), bt(b_re), bt(b_im))
    return (lbr.reshape(d, g, n), lbi.reshape(d, g, n), bbr.reshape(d, g, p, n), bbi.reshape(d, g, p, n))


def _s5_layout(lbr, lbi, bbr, bbi, c_re, c_im):
    g, n = lbr.shape
    p = bbr.shape[1]
    gb = LANES // p
    nj = g // gb
    lam = jnp.concatenate([lbr.reshape(nj, gb * n), lbi.reshape(nj, gb * n)], axis=1)
    eye = jnp.eye(gb, dtype=f32)

    def in_block(bb):
        return jnp.einsum('jgpn,gh->jgphn', bb.reshape(nj, gb, p, n), eye).reshape(nj, gb * p, gb * n)

    def out_block(c):
        return jnp.einsum('jgpn,gh->jhngp', c.reshape(nj, gb, p, n), eye).reshape(nj, gb * n, gb * p)

    wb = jnp.concatenate([in_block(bbr), in_block(bbi)], axis=2).astype(bf16)
    wc = jnp.concatenate([out_block(c_re), -out_block(c_im)], axis=1).astype(bf16)
    return lam, wb, wc


def _gelu_tanh(x):
    return 0.5 * x * (1.0 + jnp.tanh(math.sqrt(2.0 / math.pi) * (x + 0.044715 * (x * x * x))))


def _s5_tail(y, wg_ref, bg_ref):
    yg = _gelu_tanh(y)
    gate = jnp.dot(yg.astype(bf16), wg_ref[...].astype(bf16), preferred_element_type=f32) + bg_ref[...]
    return yg * _sigmoid(gate)


def _s5_kernel(u_ref, h0_ref, lam_ref, wb_ref, wc_ref, d_ref, wg_ref, bg_ref, y_ref, hfin_ref,
               hs_ref, st_ref, *, T, NJ, NC):
    @pl.when(pl.program_id(1) == 0)
    def _():
        st_ref[...] = h0_ref[...]

    u = u_ref[...]
    ub = u.astype(bf16)
    for j in range(NJ):
        r = jnp.dot(ub[:, LANES * j:LANES * (j + 1)], wb_ref[j], preferred_element_type=f32)
        for c in range(NC):
            hs_ref[c, pl.ds(j, T, stride=NJ), :] = r[:, LANES * c:LANES * (c + 1)]
    half = NC // 2
    lam_r = [lam_ref[:, LANES * c:LANES * (c + 1)] for c in range(half)]
    lam_i = [lam_ref[:, LANES * (half + c):LANES * (half + c + 1)] for c in range(half)]

    def body(t, h):
        r0 = pl.multiple_of(t * NJ, NJ)
        new = [None] * NC
        for c in range(half):
            hr, hi = h[c], h[half + c]
            nr = lam_r[c] * hr - lam_i[c] * hi + hs_ref[c, pl.ds(r0, NJ), :]
            ni = lam_r[c] * hi + lam_i[c] * hr + hs_ref[half + c, pl.ds(r0, NJ), :]
            hs_ref[c, pl.ds(r0, NJ), :] = nr
            hs_ref[half + c, pl.ds(r0, NJ), :] = ni
            new[c], new[half + c] = nr, ni
        return tuple(new)

    h = lax.fori_loop(0, T, body, tuple(st_ref[:, LANES * c:LANES * (c + 1)] for c in range(NC)),
                      unroll=8 if T % 8 == 0 else 1)
    for c in range(NC):
        st_ref[:, LANES * c:LANES * (c + 1)] = h[c]
    hfin_ref[...] = st_ref[...]
    ys = []
    for j in range(NJ):
        hj = jnp.concatenate([hs_ref[c, pl.ds(j, T, stride=NJ), :] for c in range(NC)], axis=1)
        ys.append(jnp.dot(hj.astype(bf16), wc_ref[j], preferred_element_type=f32))
    y = jnp.concatenate(ys, axis=1) + d_ref[...] * u
    y_ref[...] = _s5_tail(y, wg_ref, bg_ref).astype(y_ref.dtype)


def _s5_prompt(u, h0, lam, wb, wc, d_skip, w_glu, b_glu, b, l):
    m, c = u.shape
    nj, _, sw = wb.shape
    nc = sw // LANES
    assert nj == SUBLANES and c == nj * LANES
    t = _tile(l, 256)
    nt = l // t
    const2 = lambda shape: pl.BlockSpec(shape, lambda bi, ti: (0, 0))
    const3 = lambda shape: pl.BlockSpec(shape, lambda bi, ti: (0, 0, 0))
    y, hfin = pl.pallas_call(
        functools.partial(_s5_kernel, T=t, NJ=nj, NC=nc),
        out_shape=[SDS((m, c), bf16), SDS((b, nj, sw), f32)], grid=(b, nt),
        in_specs=[pl.BlockSpec((t, c), lambda bi, ti: (bi * nt + ti, 0)),
                  pl.BlockSpec((None, nj, sw), lambda bi, ti: (bi, 0, 0)),
                  const2((nj, sw)), const3((nj, LANES, sw)), const3((nj, sw, LANES)),
                  const2((1, c)), const2((c, c)), const2((1, c))],
        out_specs=[pl.BlockSpec((t, c), lambda bi, ti: (bi * nt + ti, 0)),
                   pl.BlockSpec((None, nj, sw), lambda bi, ti: (bi, 0, 0))],
        scratch_shapes=[pltpu.VMEM((nc, t * nj, LANES), f32), pltpu.VMEM((nj, sw), f32)],
        compiler_params=_cparams("parallel", "arbitrary"), name="s5_prompt",
    )(u, h0, lam, wb, wc, d_skip.reshape(1, c), w_glu, b_glu.reshape(1, c))
    return y, hfin


def _s5_step_kernel(u_ref, h0_ref, lam_ref, wb_ref, wc_ref, d_ref, wg_ref, bg_ref, y_ref, hn_ref, *, NJ):
    u = u_ref[...]
    ub = u.astype(bf16)
    half = lam_ref.shape[1] // 2
    ys = []
    for j in range(NJ):
        r = jnp.dot(ub[:, LANES * j:LANES * (j + 1)], wb_ref[j], preferred_element_type=f32)
        lr, li = lam_ref[j:j + 1, 0:half], lam_ref[j:j + 1, half:]
        hr, hi = h0_ref[j, :, 0:half], h0_ref[j, :, half:]
        nr = lr * hr - li * hi + r[:, 0:half]
        ni = lr * hi + li * hr + r[:, half:]
        hn_ref[j, :, 0:half] = nr
        hn_ref[j, :, half:] = ni
        hj = jnp.concatenate([nr, ni], axis=1).astype(bf16)
        ys.append(jnp.dot(hj, wc_ref[j], preferred_element_type=f32))
    y = jnp.concatenate(ys, axis=1) + d_ref[...] * u
    y_ref[...] = _s5_tail(y, wg_ref, bg_ref).astype(y_ref.dtype)


def _s5_step(u, h0, lam, wb, wc, d_skip, w_glu, b_glu):
    nb, c = u.shape
    nj, _, sw = wb.shape
    return pl.pallas_call(
        functools.partial(_s5_step_kernel, NJ=nj),
        out_shape=[SDS((nb, c), bf16), SDS((nj, nb, sw), f32)],
        compiler_params=_cparams(), name="s5_step",
    )(u, h0, lam, wb, wc, d_skip.reshape(1, c), w_glu, b_glu.reshape(1, c))


def _stage_weights(w_refs, wb_refs):
    @pl.when(pl.program_id(1) == 0)
    def _():
        for w_ref, wb_ref in zip(w_refs, wb_refs):
            wb_ref[...] = w_ref[...].astype(bf16)


def _merge_kernel(a_ref, o_ref, y_ref, wa_ref, wo_ref, wy_ref, g0_ref, g1_ref, g2_ref, m_ref,
                  wab_ref, wob_ref, wyb_ref):
    _stage_weights((wa_ref, wo_ref, wy_ref), (wab_ref, wob_ref, wyb_ref))
    ba = jnp.dot(a_ref[...], wab_ref[...], preferred_element_type=f32)
    bb = jnp.dot(o_ref[...], wob_ref[...], preferred_element_type=f32)
    bc = jnp.dot(y_ref[...], wyb_ref[...], preferred_element_type=f32)
    mix = _sigmoid(g0_ref[...]) * ba + _sigmoid(g1_ref[...]) * bb + _sigmoid(g2_ref[...]) * bc
    m_ref[...] = mix.astype(m_ref.dtype)


def _merge(a, o, y, w_conv, w_attn, w_ssm, layer, gates):
    m = a.shape[0]
    d = w_conv.shape[2]
    tm = _tile(m, 512)
    tn = _tile(d, 512)
    nb = d // tn
    lhs = lambda x: pl.BlockSpec((tm, x.shape[1]), lambda j, i: (i, 0))
    wsp = lambda w: pl.BlockSpec((None, w.shape[1], tn), lambda j, i: (layer, 0, j))
    gate = lambda gi: pl.BlockSpec((tm, tn), lambda j, i: (i, j + gi * nb))
    return pl.pallas_call(
        _merge_kernel, out_shape=SDS((m, d), bf16), grid=(nb, m // tm),
        in_specs=[lhs(a), lhs(o), lhs(y), wsp(w_conv), wsp(w_attn), wsp(w_ssm), gate(0), gate(1), gate(2)],
        out_specs=pl.BlockSpec((tm, tn), lambda j, i: (i, j)),
        scratch_shapes=[pltpu.VMEM((w.shape[1], tn), bf16) for w in (w_conv, w_attn, w_ssm)],
        compiler_params=_cparams("arbitrary", "arbitrary"), name="branch_merge",
    )(a, o, y, w_conv, w_attn, w_ssm, gates, gates, gates)


def _resnorm_kernel(x_ref, t_ref, g_ref, g2_ref, xo_ref, h_ref):
    xn = x_ref[...] + _rms(t_ref[...], g_ref[...])
    xo_ref[...] = xn
    h_ref[...] = _rms(xn, g2_ref[...]).astype(h_ref.dtype)


def _resnorm_last_kernel(x_ref, t_ref, g_ref, xo_ref):
    xo_ref[...] = x_ref[...] + _rms(t_ref[...], g_ref[...])


def _resnorm(x, t, g, g_next=None):
    m, d = x.shape
    tm = _tile(m, 256)
    row = pl.BlockSpec((tm, d), lambda i: (i, 0))
    vec = pl.BlockSpec((1, d), lambda i: (0, 0))
    if g_next is None:
        return pl.pallas_call(
            _resnorm_last_kernel, out_shape=SDS((m, d), f32), grid=(m // tm,),
            in_specs=[row, row, vec], out_specs=row,
            compiler_params=_cparams("parallel"), name="resnorm_last")(x, t, g.reshape(1, d)), None
    return pl.pallas_call(
        _resnorm_kernel, out_shape=[SDS((m, d), f32), SDS((m, d), bf16)], grid=(m // tm,),
        in_specs=[row, row, vec, vec], out_specs=[row, row],
        compiler_params=_cparams("parallel"), name="resnorm")(x, t, g.reshape(1, d), g_next.reshape(1, d))


def _ffn_gu_kernel(h_ref, wg_ref, wu_ref, o_ref, wgb_ref, wub_ref):
    _stage_weights((wg_ref, wu_ref), (wgb_ref, wub_ref))
    h = h_ref[...]
    g = jnp.dot(h, wgb_ref[...], preferred_element_type=f32)
    u = jnp.dot(h, wub_ref[...], preferred_element_type=f32)
    o_ref[...] = (g * _sigmoid(g) * u).astype(o_ref.dtype)


def _ffn_gate_up(h, w_gate, w_up, layer):
    m, d = h.shape
    f = w_gate.shape[2]
    tm = _tile(m, 1024)
    tn = _tile(f, 256)
    wsp = pl.BlockSpec((None, d, tn), lambda j, i: (layer, 0, j))
    return pl.pallas_call(
        _ffn_gu_kernel, out_shape=SDS((m, f), bf16), grid=(f // tn, m // tm),
        in_specs=[pl.BlockSpec((tm, d), lambda j, i: (i, 0)), wsp, wsp],
        out_specs=pl.BlockSpec((tm, tn), lambda j, i: (i, j)),
        scratch_shapes=[pltpu.VMEM((d, tn), bf16), pltpu.VMEM((d, tn), bf16)],
        compiler_params=_cparams("arbitrary", "arbitrary"), name="ffn_gate_up")(h, w_gate, w_up)


def _run_layer(x, h, b, l, layer, p, big, s5p, conv_prev, h0, past, g_next):
    heads = p['b_forget'].shape[0]
    c_conv = p['conv_b'].shape[0]
    d_attn = big['w_branch_attn'].shape[1]
    dh = d_attn // heads
    c_ssm = p['w_glu'].shape[0]
    off_q = 2 * c_conv
    off_k = off_q + d_attn
    off_v = off_k + d_attn
    off_f = off_v + d_attn
    off_u = off_f + heads
    off_g = off_u + c_ssm
    w_in_t = big['w_in_t']

    proj = functools.partial(_mm, h, w_in_t, layer, nt=True)
    zglu, = proj([f32], off=0, n=off_q, name="in_glu")
    q, = proj([f32 if past is not None else bf16], off=off_q, n=d_attn, name="in_q")
    k, kb = proj([f32, bf16], off=off_k, n=d_attn, name="in_k")
    v, vb = proj([f32, bf16], off=off_v, n=d_attn, name="in_v")
    zf, = proj([f32], off=off_f, n=heads, name="in_f")
    u, = proj([f32], off=off_u, n=c_ssm, name="in_u")
    gates, = proj([f32], off=off_g, name="in_gates")

    lam, wb, wc = s5p
    if past is None:
        a, conv_new = _conv_prompt(zglu, conv_prev, p['conv_w'], p['conv_b'], p['ln_conv_g'], p['ln_conv_b'], b, l)
        conv_new = conv_new[:, CONV_HIST - (p['conv_w'].shape[0] - 1):, :]
        logf, cum = _logf_prompt(zf, p['b_forget'], b, l)
        o = _flash_prompt(q, kb, vb, cum, b, l, heads, dh)
        y, hfin = _s5_prompt(u, h0, lam, wb, wc, p['ssm_d'].reshape(-1), p['w_glu'], p['b_glu'], b, l)
    else:
        a, conv_new = _conv_step(zglu, conv_prev, p['conv_w'], p['conv_b'], p['ln_conv_g'], p['ln_conv_b'])
        logf = _logf_step(zf, p['b_forget'])
        cache_k, cache_v, logf_flat, page_table = past
        o = _decode_attention(q.reshape(b, heads, dh), k.reshape(b, heads, dh), v.reshape(b, heads, dh), logf,
                              cache_k, cache_v, logf_flat, page_table, layer).reshape(b, d_attn)
        y, hn = _s5_step(u, jnp.transpose(h0, (1, 0, 2)), lam, wb, wc, p['ssm_d'].reshape(-1),
                         p['w_glu'], p['b_glu'])
        hfin = jnp.transpose(hn, (1, 0, 2))
    logf = logf.reshape(b, l, heads)

    mix = _merge(a, o, y, big['w_branch_conv'], big['w_branch_attn'], big['w_branch_ssm'], layer, gates)
    t, = _mm(mix, big['w_out'], layer, [f32], name="w_out")
    x1, hf = _resnorm(x, t, p['g_mix_post'], p['g_ffn_pre'])
    act = _ffn_gate_up(hf, big['w_ffn_gate'], big['w_ffn_up'], layer)
    f, = _mm(act, big['w_ffn_down'], layer, [f32], tm=1024, tn=256, weights_outer=False, name="ffn_down")
    x2, h_next = _resnorm(x1, f, p['g_ffn_post'], g_next)

    n_groups, n_state = p['ssm_lambda_re'].shape
    half = hfin.shape[2] // 2
    h_re = hfin[:, :, :half].reshape(b, n_groups, n_state)
    h_im = hfin[:, :, half:].reshape(b, n_groups, n_state)
    return x2, h_next, (k.reshape(b, l, heads, dh), v.reshape(b, l, heads, dh), logf, conv_new, h_re, h_im)


def _pack_state(h_re, h_im, nj):
    b = h_re.shape[0]
    return jnp.concatenate([h_re.reshape(b, nj, -1), h_im.reshape(b, nj, -1)], axis=2)


def kernel(x_prompt, x_sample, cache_k, cache_v, cache_logf, state_conv, state_ssm_re, state_ssm_im, page_table, g_mix_pre, w_in, b_forget, conv_w, conv_b, ln_conv_g, ln_conv_b, ssm_lambda_re, ssm_lambda_im, ssm_log_dt, ssm_b_re, ssm_b_im, ssm_c_re, ssm_c_im, ssm_d, w_glu, b_glu, w_branch_conv, w_branch_attn, w_branch_ssm, w_out, g_mix_post, g_ffn_pre, w_ffn_gate, w_ffn_up, w_ffn_down, g_ffn_post):
    small = dict(g_mix_pre=g_mix_pre, b_forget=b_forget, conv_w=conv_w, conv_b=conv_b,
                 ln_conv_g=ln_conv_g, ln_conv_b=ln_conv_b, ssm_lambda_re=ssm_lambda_re, ssm_d=ssm_d,
                 w_glu=w_glu, b_glu=b_glu, g_mix_post=g_mix_post, g_ffn_pre=g_ffn_pre, g_ffn_post=g_ffn_post)
    big = dict(w_in_t=jnp.swapaxes(w_in, 1, 2), w_branch_conv=w_branch_conv, w_branch_attn=w_branch_attn,
               w_branch_ssm=w_branch_ssm, w_out=w_out, w_ffn_gate=w_ffn_gate, w_ffn_up=w_ffn_up,
               w_ffn_down=w_ffn_down.astype(bf16))
    depth = w_in.shape[0]
    bp, lp, d = x_prompt.shape
    bs, ls, _ = x_sample.shape
    assert ls == 1
    n_groups, n_state = ssm_lambda_re.shape[1:]
    p_ssm = ssm_b_re.shape[-1]
    nj = n_groups * p_ssm // LANES
    c_conv = conv_b.shape[1]

    lbr, lbi, bbr, bbi = _discretize(ssm_lambda_re, ssm_lambda_im, ssm_log_dt, ssm_b_re, ssm_b_im)
    logf_flat = _flatten_logf(cache_logf)

    y_p = x_prompt.reshape(bp * lp, d)
    y_s = x_sample.reshape(bs * ls, d)
    h_p = _rmsnorm(y_p, g_mix_pre[0])
    h_s = _rmsnorm(y_s, g_mix_pre[0])
    outs_p, outs_s = [], []
    for layer in range(depth):
        p = {name: val[layer] for name, val in small.items()}
        g_next = g_mix_pre[layer + 1] if layer + 1 < depth else None
        s5p = _s5_layout(lbr[layer], lbi[layer], bbr[layer], bbi[layer], ssm_c_re[layer], ssm_c_im[layer])
        conv0 = jnp.zeros((bp, CONV_HIST, c_conv), f32)
        h0 = jnp.zeros((bp, nj, 2 * n_groups * n_state // nj), f32)
        y_p, h_p, state = _run_layer(y_p, h_p, bp, lp, layer, p, big, s5p, conv0, h0, None, g_next)
        outs_p.append(state)
        past = (cache_k, cache_v, logf_flat, page_table)
        y_s, h_s, state = _run_layer(y_s, h_s, bs, ls, layer, p, big, s5p, state_conv[layer],
                                     _pack_state(state_ssm_re[layer], state_ssm_im[layer], nj), past, g_next)
        outs_s.append(state)
    stack = lambda outs, i: jnp.stack([o[i] for o in outs])
    return ((y_p.reshape(bp, lp, d), y_s.reshape(bs, ls, d))
            + tuple(stack(outs_p, i) for i in range(6))
            + tuple(stack(outs_s, i) for i in range(6)))
```

```python
import functools
import math

import jax
import jax.numpy as jnp
from jax import lax
from jax.experimental import pallas as pl
from jax.experimental.pallas import tpu as pltpu

f32 = jnp.float32
bf16 = jnp.bfloat16
SDS = jax.ShapeDtypeStruct

LANES = 128
SUBLANES = 8
VMEM_LIMIT_BYTES = 56 * 2 ** 20
EPS = 1e-6
NEG_INF = -1e30
CONV_HIST = 32


def _cparams(*sem):
    return pltpu.CompilerParams(dimension_semantics=sem, vmem_limit_bytes=VMEM_LIMIT_BYTES)


def _sigmoid(x):
    return 1.0 / (1.0 + jnp.exp(-x))


def _tile(n, pref):
    if n <= pref:
        return n
    t = pref
    while n % t:
        t //= 2
    return t


def _rms(x, g):
    return x * lax.rsqrt(jnp.mean(x * x, axis=-1, keepdims=True) + EPS) * g


def _rmsnorm_kernel(x_ref, g_ref, o_ref):
    o_ref[...] = _rms(x_ref[...], g_ref[...]).astype(o_ref.dtype)


def _rmsnorm(x, g):
    m, d = x.shape
    tm = _tile(m, 256)
    return pl.pallas_call(
        _rmsnorm_kernel, out_shape=SDS((m, d), bf16), grid=(m // tm,),
        in_specs=[pl.BlockSpec((tm, d), lambda i: (i, 0)), pl.BlockSpec((1, d), lambda i: (0, 0))],
        out_specs=pl.BlockSpec((tm, d), lambda i: (i, 0)),
        compiler_params=_cparams("parallel"), name="rmsnorm")(x, g.reshape(1, d))


NT_DIMS = (((1,), (1,)), ((), ()))
ROW_ALIGN = 16


def _mm_kernel(x_ref, w_ref, *rest, n_out, nt):
    o_refs = rest[len(rest) - n_out:]
    w = (w_ref[0] if nt else w_ref[...]).astype(bf16)
    if nt:
        acc = lax.dot_general(x_ref[...], w, NT_DIMS, preferred_element_type=f32)
    else:
        acc = jnp.dot(x_ref[...], w, preferred_element_type=f32)
    for o in o_refs:
        o[...] = acc.astype(o.dtype)


def _mm(x, w, layer, out_dtypes, *, nt=False, off=0, n=None, tm=1024, tn=512, weights_outer=True,
        stack_first=None, name="mm"):
    m, k = x.shape
    n_total = w.shape[1] if nt else w.shape[2]
    n = n_total - off if n is None else n
    tm = _tile(m, tm)
    tn = _tile(n, tn)
    assert n % tn == 0
    if weights_outer:
        grid = (n // tn, m // tm)
        jj = lambda a, b: a
        ii = lambda a, b: b
        xspec = pl.BlockSpec((tm, k), lambda a, b: (b, 0))
    else:
        grid = (m // tm, n // tn)
        jj = lambda a, b: b
        ii = lambda a, b: a
        xspec = pl.BlockSpec((tm, k), lambda a, b: (a, 0), pipeline_mode=pl.Buffered(1))
    if nt:
        assert off % ROW_ALIGN == 0 and tn % ROW_ALIGN == 0
        wspec = pl.BlockSpec((pl.Element(1), pl.Element(tn), pl.Element(k)),
                             lambda a, b: (layer, pl.multiple_of(off + jj(a, b) * tn, ROW_ALIGN), 0))
    else:
        assert off % tn == 0
        wspec = pl.BlockSpec((None, k, tn), lambda a, b: (layer, 0, jj(a, b) + off // tn))
    out_shape = [SDS((m, n), dt) for dt in out_dtypes]
    out_specs = [pl.BlockSpec((tm, tn), lambda a, b: (ii(a, b), jj(a, b))) for _ in out_dtypes]
    operands, in_specs, aliases = [x, w], [xspec, wspec], {}
    if stack_first is not None:
        depth, prev = stack_first
        out_shape[0] = SDS((depth, m, n), out_dtypes[0])
        out_specs[0] = pl.BlockSpec((None, tm, tn), lambda a, b: (layer, ii(a, b), jj(a, b)))
        if prev is not None:
            operands.append(prev)
            in_specs.append(pl.BlockSpec(memory_space=pl.ANY))
            aliases = {2: 0}
    return pl.pallas_call(
        functools.partial(_mm_kernel, n_out=len(out_dtypes), nt=nt),
        out_shape=out_shape, grid=grid, in_specs=in_specs, out_specs=out_specs,
        input_output_aliases=aliases,
        compiler_params=_cparams("parallel", "parallel"), name=name)(*operands)


def _conv_kernel(za_ref, zb_ref, prev_ref, w_ref, cb_ref, lg_ref, lb_ref, o_ref, new_ref,
                 ext_ref, conv_ref, *, T, W):
    t = pl.program_id(1)

    nc = cb_ref.shape[1]

    @pl.when(t == 0)
    def _():
        ext_ref[0:CONV_HIST, :] = prev_ref[...]
        ext_ref[CONV_HIST + T:CONV_HIST + T + SUBLANES, :] = jnp.zeros((SUBLANES, nc), f32)

    @pl.when(t > 0)
    def _():
        ext_ref[0:CONV_HIST, :] = ext_ref[T:T + CONV_HIST, :]

    ext_ref[CONV_HIST:CONV_HIST + T, :] = za_ref[...] * _sigmoid(zb_ref[...])
    rc = min(T, 32)
    base = CONV_HIST - (W - 1)
    for r0 in range(0, T, rc):
        acc = jnp.broadcast_to(cb_ref[...], (rc, nc))
        for s in range(SUBLANES):
            part = None
            for k in range(W):
                if (base + k) % SUBLANES != s:
                    continue
                a0 = r0 + base + k - s
                term = w_ref[k:k + 1, :] * ext_ref[a0:a0 + rc + (SUBLANES if s else 0), :]
                part = term if part is None else part + term
            if part is not None:
                acc = acc + part[s:s + rc, :]
        conv_ref[r0:r0 + rc, :] = acc
    c = conv_ref[...]
    mu = jnp.mean(c, axis=-1, keepdims=True)
    xc = c - mu
    y = xc * lax.rsqrt(jnp.mean(xc * xc, axis=-1, keepdims=True) + EPS) * lg_ref[...] + lb_ref[...]
    o_ref[...] = (y * _sigmoid(y)).astype(o_ref.dtype)
    new_ref[...] = ext_ref[T:T + CONV_HIST, :]


def _conv_prompt(zglu, prev32, conv_w, conv_b, ln_g, ln_b, b, l):
    m, c2 = zglu.shape
    c = c2 // 2
    w = conv_w.shape[0]
    t = _tile(l, 256)
    nt = l // t
    kern = functools.partial(_conv_kernel, T=t, W=w)
    row = lambda bi, ti: (bi * nt + ti, 0)
    vec = pl.BlockSpec((1, c), lambda bi, ti: (0, 0))
    return pl.pallas_call(
        kern, out_shape=[SDS((m, c), bf16), SDS((b, CONV_HIST, c), f32)], grid=(b, nt),
        in_specs=[pl.BlockSpec((t, c), row), pl.BlockSpec((t, c), lambda bi, ti: (bi * nt + ti, 1)),
                  pl.BlockSpec((None, CONV_HIST, c), lambda bi, ti: (bi, 0, 0)),
                  pl.BlockSpec((w, c), lambda bi, ti: (0, 0)), vec, vec, vec],
        out_specs=[pl.BlockSpec((t, c), row), pl.BlockSpec((None, CONV_HIST, c), lambda bi, ti: (bi, 0, 0))],
        scratch_shapes=[pltpu.VMEM((t + CONV_HIST + SUBLANES, c), f32), pltpu.VMEM((t, c), f32)],
        compiler_params=_cparams("parallel", "arbitrary"), name="conv_prompt",
    )(zglu, zglu, prev32, conv_w, conv_b.reshape(1, c), ln_g.reshape(1, c), ln_b.reshape(1, c))


def _conv_step_kernel(zg_ref, st_ref, w_ref, cb_ref, lg_ref, lb_ref, o_ref, new_ref, *, W):
    c = cb_ref.shape[1]
    nb = zg_ref.shape[0]
    a = zg_ref[:, 0:c] * _sigmoid(zg_ref[:, c:2 * c])
    rows = []
    for bi in range(nb):
        hist = st_ref[bi]
        conv = (jnp.sum(hist * w_ref[0:W - 1, :], axis=0, keepdims=True)
                + w_ref[W - 1:W, :] * a[bi:bi + 1, :] + cb_ref[...])
        rows.append(conv)
        new_ref[bi, 0:W - 2, :] = st_ref[bi, 1:W - 1, :]
        new_ref[bi, W - 2:W - 1, :] = a[bi:bi + 1, :]
    cv = jnp.concatenate(rows, axis=0)
    mu = jnp.mean(cv, axis=-1, keepdims=True)
    xc = cv - mu
    y = xc * lax.rsqrt(jnp.mean(xc * xc, axis=-1, keepdims=True) + EPS) * lg_ref[...] + lb_ref[...]
    o_ref[...] = (y * _sigmoid(y)).astype(o_ref.dtype)


def _conv_step(zglu, state, conv_w, conv_b, ln_g, ln_b):
    nb, c2 = zglu.shape
    c = c2 // 2
    w = conv_w.shape[0]
    return pl.pallas_call(
        functools.partial(_conv_step_kernel, W=w),
        out_shape=[SDS((nb, c), bf16), SDS((nb, w - 1, c), f32)],
        compiler_params=_cparams(), name="conv_step",
    )(zglu, state, conv_w, conv_b.reshape(1, c), ln_g.reshape(1, c), ln_b.reshape(1, c))


def _log_sigmoid(z):
    return jnp.minimum(z, 0.0) - jnp.log1p(jnp.exp(-jnp.abs(z)))


def _logf_kernel(zf_ref, b_ref, lf_ref, cum_ref, *, H):
    lf = _log_sigmoid(zf_ref[:, 0:H] + b_ref[...])
    lf_ref[...] = lf
    n = lf.shape[0]
    row = lax.broadcasted_iota(jnp.int32, lf.shape, 0)
    x = lf
    s = 1
    while s < n:
        x = x + jnp.where(row >= s, pltpu.roll(x, s, axis=0), 0.0)
        s *= 2
    cum_ref[...] = x


def _logf_prompt(zf, b_forget, b, l):
    h = b_forget.shape[0]
    return pl.pallas_call(
        functools.partial(_logf_kernel, H=h),
        out_shape=[SDS((b, l, h), f32), SDS((b, l, h), f32)], grid=(b,),
        in_specs=[pl.BlockSpec((l, zf.shape[1]), lambda i: (i, 0)), pl.BlockSpec((1, h), lambda i: (0, 0))],
        out_specs=[pl.BlockSpec((None, l, h), lambda i: (i, 0, 0))] * 2,
        compiler_params=_cparams("parallel"), name="logf_prompt")(zf, b_forget.reshape(1, h))


def _logf_step_kernel(zf_ref, b_ref, lf_ref, *, H):
    lf_ref[...] = _log_sigmoid(zf_ref[:, 0:H] + b_ref[...])


def _logf_step(zf, b_forget):
    h = b_forget.shape[0]
    return pl.pallas_call(functools.partial(_logf_step_kernel, H=h), out_shape=SDS((zf.shape[0], h), f32),
                          compiler_params=_cparams(), name="logf_step")(zf, b_forget.reshape(1, h))


def _flash_kernel(q_ref, k_ref, v_ref, cc_ref, cr_ref, o_ref, *, tq, scale, H):
    h = pl.program_id(1)
    n = q_ref.shape[0]
    nt = (((1,), (1,)), ((), ()))
    lane = lax.broadcasted_iota(jnp.int32, (n, H), 1)
    cq = jnp.sum(jnp.where(lane == h, cc_ref[...], 0.0), axis=1, keepdims=True)
    rr = lax.broadcasted_iota(jnp.int32, (tq, tq), 0)
    cc = lax.broadcasted_iota(jnp.int32, (tq, tq), 1)
    for i in range(n // tq):
        lo, hi = i * tq, (i + 1) * tq
        q = q_ref[lo:hi, :]
        cqi = cq[lo:hi, :]
        sd = lax.dot_general(q, k_ref[lo:hi, :], nt, preferred_element_type=f32) * scale
        sd = sd + cqi - cr_ref[:, lo:hi]
        sd = jnp.where(cc <= rr, sd, NEG_INF)
        m = jnp.max(sd, axis=1, keepdims=True)
        if i > 0:
            so = lax.dot_general(q, k_ref[0:lo, :], nt, preferred_element_type=f32) * scale
            so = so + cqi - cr_ref[:, 0:lo]
            m = jnp.maximum(m, jnp.max(so, axis=1, keepdims=True))
            po = jnp.exp(so - m)
            den = jnp.sum(po, axis=1, keepdims=True)
            acc = jnp.dot(po.astype(bf16), v_ref[0:lo, :], preferred_element_type=f32)
        pd = jnp.exp(sd - m)
        if i > 0:
            den = den + jnp.sum(pd, axis=1, keepdims=True)
            acc = acc + jnp.dot(pd.astype(bf16), v_ref[lo:hi, :], preferred_element_type=f32)
        else:
            den = jnp.sum(pd, axis=1, keepdims=True)
            acc = jnp.dot(pd.astype(bf16), v_ref[lo:hi, :], preferred_element_type=f32)
        o_ref[lo:hi, :] = (acc / den).astype(o_ref.dtype)


def _flash_prompt(q, k, v, cum, b, l, h, dh):
    m = q.shape[0]
    tq = _tile(l, 512)
    cum_row = jnp.transpose(cum, (0, 2, 1)).reshape(b, h, 1, l)
    blk = pl.BlockSpec((l, dh), lambda bi, hi: (bi, hi))
    return pl.pallas_call(
        functools.partial(_flash_kernel, tq=tq, scale=dh ** -0.5, H=h),
        out_shape=SDS((m, h * dh), bf16), grid=(b, h),
        in_specs=[blk, blk, blk,
                  pl.BlockSpec((None, l, h), lambda bi, hi: (bi, 0, 0)),
                  pl.BlockSpec((None, None, 1, l), lambda bi, hi: (bi, hi, 0, 0))],
        out_specs=blk,
        compiler_params=_cparams("parallel", "arbitrary"), name="flash_prompt")(q, k, v, cum, cum_row)


PAGE_GROUP = SUBLANES


def _decode_kernel(pt_ref, q_ref, kn_ref, vn_ref, lfn_ref, *rest, scale, NPG, G):
    k_refs, v_refs, f_refs = rest[0:G], rest[G:2 * G], rest[2 * G:3 * G]
    o_ref = rest[3 * G]
    m_s, l_s, acc_s, pref_s = rest[3 * G + 1:]
    bi = pl.program_id(0)
    p = pl.program_id(1)
    r, h, dh = k_refs[0].shape
    n = r * h

    @pl.when(p == 0)
    def _():
        m_s[...] = jnp.full(m_s.shape, NEG_INF, f32)
        l_s[...] = jnp.zeros(l_s.shape, f32)
        acc_s[...] = jnp.zeros(acc_s.shape, f32)
        pref_s[...] = jnp.zeros(pref_s.shape, f32)

    rows = []
    for g in range(G):
        pg = pt_ref[bi * NPG + p * G + g]
        rows.append(f_refs[g][pl.ds(jnp.bitwise_and(pg, PAGE_GROUP - 1), 1), :])
    lf = jnp.concatenate(rows, axis=0) if G > 1 else rows[0]
    lane = lax.broadcasted_iota(jnp.int32, lf.shape, 1)
    cw, tot = lf, lf
    s = h
    while s < n:
        cw = cw + jnp.where(lane >= s, pltpu.roll(cw, s, axis=1), 0.0)
        tot = tot + pltpu.roll(tot, s, axis=1)
        s *= 2
    off = pref_s[...]
    biases = []
    for g in range(G):
        biases.append(off + cw[g:g + 1, :])
        off = off + tot[g:g + 1, :]
    pref_s[...] = off
    bias = jnp.concatenate(biases, axis=1) if G > 1 else biases[0]

    q = q_ref[...]
    qb = q.astype(bf16)
    sts = [lax.dot_general(qb, k_refs[g][...].reshape(n, dh).astype(bf16), NT_DIMS,
                           preferred_element_type=f32) for g in range(G)]
    st = (jnp.concatenate(sts, axis=1) if G > 1 else sts[0]) * scale
    s = st - bias
    lane = lax.broadcasted_iota(jnp.int32, s.shape, 1)
    sub = lax.broadcasted_iota(jnp.int32, s.shape, 0)
    s = jnp.where(jnp.bitwise_and(lane, h - 1) == sub, s, NEG_INF)
    m_old = m_s[...]
    m_new = jnp.maximum(m_old, jnp.max(s, axis=1, keepdims=True))
    alpha = jnp.exp(m_old - m_new)
    pr = jnp.exp(s - m_new)
    l_s[...] = alpha * l_s[...] + jnp.sum(pr, axis=1, keepdims=True)
    acc = alpha * acc_s[...]
    for g in range(G):
        acc = acc + jnp.dot(pr[:, g * n:(g + 1) * n].astype(bf16), v_refs[g][...].reshape(n, dh).astype(bf16),
                            preferred_element_type=f32)
    acc_s[...] = acc
    m_s[...] = m_new

    @pl.when(p == NPG // G - 1)
    def _():
        cn_row = pref_s[:, 0:h] + lfn_ref[...]
        eye = lax.broadcasted_iota(jnp.int32, (h, h), 0) == lax.broadcasted_iota(jnp.int32, (h, h), 1)
        cn = jnp.sum(jnp.where(eye, jnp.broadcast_to(cn_row, (h, h)), 0.0), axis=1, keepdims=True)
        s_new = jnp.sum(q * kn_ref[...], axis=1, keepdims=True) * scale - cn
        m_o = m_s[...]
        m_f = jnp.maximum(m_o, s_new)
        a = jnp.exp(m_o - m_f)
        pn = jnp.exp(s_new - m_f)
        den = a * l_s[...] + pn
        o_ref[...] = ((a * acc_s[...] + pn * vn_ref[...]) / den).astype(o_ref.dtype)


def _flatten_logf(cache_logf):
    d, n_pool, r, h = cache_logf.shape
    pad = -n_pool % PAGE_GROUP
    flat = cache_logf.reshape(d, n_pool, r * h)
    if pad:
        flat = jnp.pad(flat, ((0, 0), (0, pad), (0, 0)))
    return flat.reshape(d, (n_pool + pad) // PAGE_GROUP, PAGE_GROUP, r * h)


def _decode_attention(q, k_new, v_new, lf_new, cache_k, cache_v, logf_flat, page_table, layer):
    nb, h, dh = q.shape
    _, _, r, _, _ = cache_k.shape
    npg = page_table.shape[1]
    assert h & (h - 1) == 0 and h % SUBLANES == 0 and r & (r - 1) == 0
    g = max(c for c in (8, 4, 2, 1) if npg % c == 0)
    pt = page_table.reshape(-1).astype(jnp.int32)
    page_id = lambda bi, pi, pt_ref, gi: pt_ref[bi * npg + pi * g + gi]
    new = pl.BlockSpec((None, h, dh), lambda bi, pi, pt_ref: (bi, 0, 0))
    page = [pl.BlockSpec((None, None, r, h, dh),
                         functools.partial(lambda bi, pi, pt_ref, gi: (layer, page_id(bi, pi, pt_ref, gi), 0, 0, 0), gi=gi))
            for gi in range(g)]
    flat = [pl.BlockSpec((None, None, PAGE_GROUP, r * h),
                         functools.partial(lambda bi, pi, pt_ref, gi:
                                           (layer, page_id(bi, pi, pt_ref, gi) // PAGE_GROUP, 0, 0), gi=gi))
            for gi in range(g)]
    grid_spec = pltpu.PrefetchScalarGridSpec(
        num_scalar_prefetch=1, grid=(nb, npg // g),
        in_specs=[new, new, new, pl.BlockSpec((None, 1, h), lambda bi, pi, pt_ref: (bi, 0, 0))]
        + page + page + flat,
        out_specs=new,
        scratch_shapes=[pltpu.VMEM((h, 1), f32), pltpu.VMEM((h, 1), f32), pltpu.VMEM((h, dh), f32),
                        pltpu.VMEM((1, r * h), f32)])
    return pl.pallas_call(
        functools.partial(_decode_kernel, scale=dh ** -0.5, NPG=npg, G=g),
        out_shape=SDS((nb, h, dh), bf16), grid_spec=grid_spec,
        compiler_params=_cparams("parallel", "arbitrary"), name="decode_attention",
    )(pt, q, k_new, v_new, lf_new.reshape(nb, 1, h), *([cache_k] * g), *([cache_v] * g), *([logf_flat] * g))


def _disc_kernel(lr_ref, li_ref, ldt_ref, br_ref, bi_ref, lbr_ref, lbi_ref, bbr_ref, bbi_ref, cr_s, ci_s):
    lr = lr_ref[...]
    li = li_ref[...]
    dt = jnp.exp(ldt_ref[...])
    er = jnp.exp(lr * dt)
    ang = li * dt
    lbr = er * jnp.cos(ang)
    lbi = er * jnp.sin(ang)
    lbr_ref[...] = lbr
    lbi_ref[...] = lbi
    nr = lbr - 1.0
    den = lr * lr + li * li
    cr_s[...] = (nr * lr + lbi * li) / den
    ci_s[...] = (lbi * lr - nr * li) / den
    for g in range(lr.shape[0]):
        cr = cr_s[g:g + 1, :]
        ci = ci_s[g:g + 1, :]
        br = br_ref[g]
        bi = bi_ref[g]
        bbr_ref[g] = cr * br - ci * bi
        bbi_ref[g] = cr * bi + ci * br


def _discretize(lam_re, lam_im, log_dt, b_re, b_im):
    d, g, n = lam_re.shape
    p = b_re.shape[-1]
    dg = d * g
    bt = lambda x: jnp.transpose(x, (0, 1, 3, 2)).reshape(dg, p, n)
    lbr, lbi, bbr, bbi = pl.pallas_call(
        _disc_kernel,
        out_shape=[SDS((dg, n), f32), SDS((dg, n), f32), SDS((dg, p, n), f32), SDS((dg, p, n), f32)],
        scratch_shapes=[pltpu.VMEM((dg, n), f32), pltpu.VMEM((dg, n), f32)],
        compiler_params=_cparams(), name="s5_discretize",
    )(lam_re.reshape(dg, n), lam_im.reshape(dg, n), log_dt.reshape(dg, 1), bt(b_re), bt(b_im))
    return (lbr.reshape(d, g, n), lbi.reshape(d, g, n), bbr.reshape(d, g, p, n), bbi.reshape(d, g, p, n))


def _s5_layout(lbr, lbi, bbr, bbi, c_re, c_im):
    g, n = lbr.shape
    p = bbr.shape[1]
    gb = LANES // p
    nj = g // gb
    lam = jnp.concatenate([lbr.reshape(nj, gb * n), lbi.reshape(nj, gb * n)], axis=1)
    eye = jnp.eye(gb, dtype=f32)

    def in_block(bb):
        return jnp.einsum('jgpn,gh->jgphn', bb.reshape(nj, gb, p, n), eye).reshape(nj, gb * p, gb * n)

    def out_block(c):
        return jnp.einsum('jgpn,gh->jhngp', c.reshape(nj, gb, p, n), eye).reshape(nj, gb * n, gb * p)

    wb = jnp.concatenate([in_block(bbr), in_block(bbi)], axis=2).astype(bf16)
    wc = jnp.concatenate([out_block(c_re), -out_block(c_im)], axis=1).astype(bf16)
    return lam, wb, wc


def _gelu_tanh(x):
    return 0.5 * x * (1.0 + jnp.tanh(math.sqrt(2.0 / math.pi) * (x + 0.044715 * (x * x * x))))


def _s5_tail(y, wg_ref, bg_ref):
    yg = _gelu_tanh(y)
    gate = jnp.dot(yg.astype(bf16), wg_ref[...].astype(bf16), preferred_element_type=f32) + bg_ref[...]
    return yg * _sigmoid(gate)


def _s5_kernel(u_ref, h0_ref, lam_ref, wb_ref, wc_ref, d_ref, wg_ref, bg_ref, y_ref, hfin_ref,
               hs_ref, st_ref, *, T, NJ, NC):
    @pl.when(pl.program_id(1) == 0)
    def _():
        st_ref[...] = h0_ref[...]

    u = u_ref[...]
    ub = u.astype(bf16)
    for j in range(NJ):
        r = jnp.dot(ub[:, LANES * j:LANES * (j + 1)], wb_ref[j], preferred_element_type=f32)
        for c in range(NC):
            hs_ref[c, pl.ds(j, T, stride=NJ), :] = r[:, LANES * c:LANES * (c + 1)]
    half = NC // 2
    lam_r = [lam_ref[:, LANES * c:LANES * (c + 1)] for c in range(half)]
    lam_i = [lam_ref[:, LANES * (half + c):LANES * (half + c + 1)] for c in range(half)]

    def body(t, h):
        r0 = pl.multiple_of(t * NJ, NJ)
        new = [None] * NC
        for c in range(half):
            hr, hi = h[c], h[half + c]
            nr = lam_r[c] * hr - lam_i[c] * hi + hs_ref[c, pl.ds(r0, NJ), :]
            ni = lam_r[c] * hi + lam_i[c] * hr + hs_ref[half + c, pl.ds(r0, NJ), :]
            hs_ref[c, pl.ds(r0, NJ), :] = nr
            hs_ref[half + c, pl.ds(r0, NJ), :] = ni
            new[c], new[half + c] = nr, ni
        return tuple(new)

    h = lax.fori_loop(0, T, body, tuple(st_ref[:, LANES * c:LANES * (c + 1)] for c in range(NC)),
                      unroll=8 if T % 8 == 0 else 1)
    for c in range(NC):
        st_ref[:, LANES * c:LANES * (c + 1)] = h[c]
    hfin_ref[...] = st_ref[...]
    ys = []
    for j in range(NJ):
        hj = jnp.concatenate([hs_ref[c, pl.ds(j, T, stride=NJ), :] for c in range(NC)], axis=1)
        ys.append(jnp.dot(hj.astype(bf16), wc_ref[j], preferred_element_type=f32))
    y = jnp.concatenate(ys, axis=1) + d_ref[...] * u
    y_ref[...] = _s5_tail(y, wg_ref, bg_ref).astype(y_ref.dtype)


def _s5_prompt(u, h0, lam, wb, wc, d_skip, w_glu, b_glu, b, l):
    m, c = u.shape
    nj, _, sw = wb.shape
    nc = sw // LANES
    assert nj == SUBLANES and c == nj * LANES
    t = _tile(l, 512)
    nt = l // t
    const2 = lambda shape: pl.BlockSpec(shape, lambda bi, ti: (0, 0))
    const3 = lambda shape: pl.BlockSpec(shape, lambda bi, ti: (0, 0, 0))
    y, hfin = pl.pallas_call(
        functools.partial(_s5_kernel, T=t, NJ=nj, NC=nc),
        out_shape=[SDS((m, c), bf16), SDS((b, nj, sw), f32)], grid=(b, nt),
        in_specs=[pl.BlockSpec((t, c), lambda bi, ti: (bi * nt + ti, 0)),
                  pl.BlockSpec((None, nj, sw), lambda bi, ti: (bi, 0, 0)),
                  const2((nj, sw)), const3((nj, LANES, sw)), const3((nj, sw, LANES)),
                  const2((1, c)), const2((c, c)), const2((1, c))],
        out_specs=[pl.BlockSpec((t, c), lambda bi, ti: (bi * nt + ti, 0)),
                   pl.BlockSpec((None, nj, sw), lambda bi, ti: (bi, 0, 0))],
        scratch_shapes=[pltpu.VMEM((nc, t * nj, LANES), f32), pltpu.VMEM((nj, sw), f32)],
        compiler_params=_cparams("parallel", "arbitrary"), name="s5_prompt",
    )(u, h0, lam, wb, wc, d_skip.reshape(1, c), w_glu, b_glu.reshape(1, c))
    return y, hfin


def _s5_step_kernel(u_ref, h0_ref, lam_ref, wb_ref, wc_ref, d_ref, wg_ref, bg_ref, y_ref, hn_ref, *, NJ):
    u = u_ref[...]
    ub = u.astype(bf16)
    half = lam_ref.shape[1] // 2
    ys = []
    for j in range(NJ):
        r = jnp.dot(ub[:, LANES * j:LANES * (j + 1)], wb_ref[j], preferred_element_type=f32)
        lr, li = lam_ref[j:j + 1, 0:half], lam_ref[j:j + 1, half:]
        hr, hi = h0_ref[j, :, 0:half], h0_ref[j, :, half:]
        nr = lr * hr - li * hi + r[:, 0:half]
        ni = lr * hi + li * hr + r[:, half:]
        hn_ref[j, :, 0:half] = nr
        hn_ref[j, :, half:] = ni
        hj = jnp.concatenate([nr, ni], axis=1).astype(bf16)
        ys.append(jnp.dot(hj, wc_ref[j], preferred_element_type=f32))
    y = jnp.concatenate(ys, axis=1) + d_ref[...] * u
    y_ref[...] = _s5_tail(y, wg_ref, bg_ref).astype(y_ref.dtype)


def _s5_step(u, h0, lam, wb, wc, d_skip, w_glu, b_glu):
    nb, c = u.shape
    nj, _, sw = wb.shape
    return pl.pallas_call(
        functools.partial(_s5_step_kernel, NJ=nj),
        out_shape=[SDS((nb, c), bf16), SDS((nj, nb, sw), f32)],
        compiler_params=_cparams(), name="s5_step",
    )(u, h0, lam, wb, wc, d_skip.reshape(1, c), w_glu, b_glu.reshape(1, c))


def _merge_kernel(a_ref, o_ref, y_ref, wa_ref, wo_ref, wy_ref, g0_ref, g1_ref, g2_ref, m_ref):
    ba = jnp.dot(a_ref[...], wa_ref[...].astype(bf16), preferred_element_type=f32)
    bb = jnp.dot(o_ref[...], wo_ref[...].astype(bf16), preferred_element_type=f32)
    bc = jnp.dot(y_ref[...], wy_ref[...].astype(bf16), preferred_element_type=f32)
    mix = _sigmoid(g0_ref[...]) * ba + _sigmoid(g1_ref[...]) * bb + _sigmoid(g2_ref[...]) * bc
    m_ref[...] = mix.astype(m_ref.dtype)


def _merge(a, o, y, w_conv, w_attn, w_ssm, layer, gates):
    m = a.shape[0]
    d = w_conv.shape[2]
    tm = _tile(m, 512)
    tn = _tile(d, 512)
    nb = d // tn
    lhs = lambda x: pl.BlockSpec((tm, x.shape[1]), lambda j, i: (i, 0))
    wsp = lambda w: pl.BlockSpec((None, w.shape[1], tn), lambda j, i: (layer, 0, j))
    gate = lambda gi: pl.BlockSpec((tm, tn), lambda j, i: (i, j + gi * nb))
    return pl.pallas_call(
        _merge_kernel, out_shape=SDS((m, d), bf16), grid=(nb, m // tm),
        in_specs=[lhs(a), lhs(o), lhs(y), wsp(w_conv), wsp(w_attn), wsp(w_ssm), gate(0), gate(1), gate(2)],
        out_specs=pl.BlockSpec((tm, tn), lambda j, i: (i, j)),
        compiler_params=_cparams("parallel", "parallel"), name="branch_merge",
    )(a, o, y, w_conv, w_attn, w_ssm, gates, gates, gates)


def _resnorm_kernel(x_ref, t_ref, g_ref, g2_ref, xo_ref, h_ref):
    xn = x_ref[...] + _rms(t_ref[...], g_ref[...])
    xo_ref[...] = xn
    h_ref[...] = _rms(xn, g2_ref[...]).astype(h_ref.dtype)


def _resnorm_last_kernel(x_ref, t_ref, g_ref, xo_ref):
    xo_ref[...] = x_ref[...] + _rms(t_ref[...], g_ref[...])


def _resnorm(x, t, g, g_next=None):
    m, d = x.shape
    tm = _tile(m, 256)
    row = pl.BlockSpec((tm, d), lambda i: (i, 0))
    vec = pl.BlockSpec((1, d), lambda i: (0, 0))
    if g_next is None:
        return pl.pallas_call(
            _resnorm_last_kernel, out_shape=SDS((m, d), f32), grid=(m // tm,),
            in_specs=[row, row, vec], out_specs=row,
            compiler_params=_cparams("parallel"), name="resnorm_last")(x, t, g.reshape(1, d)), None
    return pl.pallas_call(
        _resnorm_kernel, out_shape=[SDS((m, d), f32), SDS((m, d), bf16)], grid=(m // tm,),
        in_specs=[row, row, vec, vec], out_specs=[row, row],
        compiler_params=_cparams("parallel"), name="resnorm")(x, t, g.reshape(1, d), g_next.reshape(1, d))


def _ffn_gu_kernel(h_ref, wg_ref, wu_ref, o_ref):
    h = h_ref[...]
    g = jnp.dot(h, wg_ref[...].astype(bf16), preferred_element_type=f32)
    u = jnp.dot(h, wu_ref[...].astype(bf16), preferred_element_type=f32)
    o_ref[...] = (g * _sigmoid(g) * u).astype(o_ref.dtype)


def _ffn_gate_up(h, w_gate, w_up, layer):
    m, d = h.shape
    f = w_gate.shape[2]
    tm = _tile(m, 1024)
    tn = _tile(f, 256)
    wsp = pl.BlockSpec((None, d, tn), lambda j, i: (layer, 0, j))
    return pl.pallas_call(
        _ffn_gu_kernel, out_shape=SDS((m, f), bf16), grid=(f // tn, m // tm),
        in_specs=[pl.BlockSpec((tm, d), lambda j, i: (i, 0)), wsp, wsp],
        out_specs=pl.BlockSpec((tm, tn), lambda j, i: (i, j)),
        compiler_params=_cparams("parallel", "parallel"), name="ffn_gate_up")(h, w_gate, w_up)


def _run_layer(x, h, b, l, layer, p, big, s5p, conv_prev, h0, past, g_next, kv_stack=None):
    heads = p['b_forget'].shape[0]
    c_conv = p['conv_b'].shape[0]
    d_attn = big['w_branch_attn'].shape[1]
    dh = d_attn // heads
    c_ssm = p['w_glu'].shape[0]
    off_q = 2 * c_conv
    off_k = off_q + d_attn
    off_v = off_k + d_attn
    off_f = off_v + d_attn
    off_u = off_f + heads
    off_g = off_u + c_ssm
    w_in_t = big['w_in_t']

    proj = functools.partial(_mm, h, w_in_t, layer, nt=True)
    zglu, = proj([f32], off=0, n=off_q, name="in_glu")
    q, = proj([f32 if past is not None else bf16], off=off_q, n=d_attn, name="in_q")
    k_stack = v_stack = None
    if kv_stack is not None:
        depth, k_prev, v_prev = kv_stack
        k_stack = (depth, k_prev)
        v_stack = (depth, v_prev)
    k, kb = proj([f32, bf16], off=off_k, n=d_attn, stack_first=k_stack, name="in_k")
    v, vb = proj([f32, bf16], off=off_v, n=d_attn, stack_first=v_stack, name="in_v")
    zf, = proj([f32], off=off_f, n=heads, name="in_f")
    u, = proj([f32], off=off_u, n=c_ssm, name="in_u")
    gates, = proj([f32], off=off_g, name="in_gates")

    lam, wb, wc = s5p
    if past is None:
        a, conv_new = _conv_prompt(zglu, conv_prev, p['conv_w'], p['conv_b'], p['ln_conv_g'], p['ln_conv_b'], b, l)
        conv_new = conv_new[:, CONV_HIST - (p['conv_w'].shape[0] - 1):, :]
        logf, cum = _logf_prompt(zf, p['b_forget'], b, l)
        o = _flash_prompt(q, kb, vb, cum, b, l, heads, dh)
        y, hfin = _s5_prompt(u, h0, lam, wb, wc, p['ssm_d'].reshape(-1), p['w_glu'], p['b_glu'], b, l)
    else:
        a, conv_new = _conv_step(zglu, conv_prev, p['conv_w'], p['conv_b'], p['ln_conv_g'], p['ln_conv_b'])
        logf = _logf_step(zf, p['b_forget'])
        cache_k, cache_v, logf_flat, page_table = past
        o = _decode_attention(q.reshape(b, heads, dh), k.reshape(b, heads, dh), v.reshape(b, heads, dh), logf,
                              cache_k, cache_v, logf_flat, page_table, layer).reshape(b, d_attn)
        y, hn = _s5_step(u, jnp.transpose(h0, (1, 0, 2)), lam, wb, wc, p['ssm_d'].reshape(-1),
                         p['w_glu'], p['b_glu'])
        hfin = jnp.transpose(hn, (1, 0, 2))
    logf = logf.reshape(b, l, heads)

    mix = _merge(a, o, y, big['w_branch_conv'], big['w_branch_attn'], big['w_branch_ssm'], layer, gates)
    t, = _mm(mix, big['w_out'], layer, [f32], name="w_out")
    x1, hf = _resnorm(x, t, p['g_mix_post'], p['g_ffn_pre'])
    act = _ffn_gate_up(hf, big['w_ffn_gate'], big['w_ffn_up'], layer)
    f, = _mm(act, big['w_ffn_down'], layer, [f32], tm=1024, tn=256, weights_outer=False, name="ffn_down")
    x2, h_next = _resnorm(x1, f, p['g_ffn_post'], g_next)

    n_groups, n_state = p['ssm_lambda_re'].shape
    half = hfin.shape[2] // 2
    h_re = hfin[:, :, :half].reshape(b, n_groups, n_state)
    h_im = hfin[:, :, half:].reshape(b, n_groups, n_state)
    if kv_stack is None:
        k, v = k.reshape(b, l, heads, dh), v.reshape(b, l, heads, dh)
    return x2, h_next, (k, v, logf, conv_new, h_re, h_im)


def _pack_state(h_re, h_im, nj):
    b = h_re.shape[0]
    return jnp.concatenate([h_re.reshape(b, nj, -1), h_im.reshape(b, nj, -1)], axis=2)


def kernel(x_prompt, x_sample, cache_k, cache_v, cache_logf, state_conv, state_ssm_re, state_ssm_im, page_table, g_mix_pre, w_in, b_forget, conv_w, conv_b, ln_conv_g, ln_conv_b, ssm_lambda_re, ssm_lambda_im, ssm_log_dt, ssm_b_re, ssm_b_im, ssm_c_re, ssm_c_im, ssm_d, w_glu, b_glu, w_branch_conv, w_branch_attn, w_branch_ssm, w_out, g_mix_post, g_ffn_pre, w_ffn_gate, w_ffn_up, w_ffn_down, g_ffn_post):
    small = dict(g_mix_pre=g_mix_pre, b_forget=b_forget, conv_w=conv_w, conv_b=conv_b,
                 ln_conv_g=ln_conv_g, ln_conv_b=ln_conv_b, ssm_lambda_re=ssm_lambda_re, ssm_d=ssm_d,
                 w_glu=w_glu, b_glu=b_glu, g_mix_post=g_mix_post, g_ffn_pre=g_ffn_pre, g_ffn_post=g_ffn_post)
    big = dict(w_in_t=jnp.swapaxes(w_in, 1, 2), w_branch_conv=w_branch_conv, w_branch_attn=w_branch_attn,
               w_branch_ssm=w_branch_ssm, w_out=w_out, w_ffn_gate=w_ffn_gate, w_ffn_up=w_ffn_up,
               w_ffn_down=w_ffn_down.astype(bf16))
    depth = w_in.shape[0]
    bp, lp, d = x_prompt.shape
    bs, ls, _ = x_sample.shape
    assert ls == 1
    n_groups, n_state = ssm_lambda_re.shape[1:]
    p_ssm = ssm_b_re.shape[-1]
    nj = n_groups * p_ssm // LANES
    c_conv = conv_b.shape[1]

    lbr, lbi, bbr, bbi = _discretize(ssm_lambda_re, ssm_lambda_im, ssm_log_dt, ssm_b_re, ssm_b_im)
    logf_flat = _flatten_logf(cache_logf)

    y_p = x_prompt.reshape(bp * lp, d)
    y_s = x_sample.reshape(bs * ls, d)
    h_p = _rmsnorm(y_p, g_mix_pre[0])
    h_s = _rmsnorm(y_s, g_mix_pre[0])
    outs_p, outs_s = [], []
    k_all = v_all = None
    for layer in range(depth):
        p = {name: val[layer] for name, val in small.items()}
        g_next = g_mix_pre[layer + 1] if layer + 1 < depth else None
        s5p = _s5_layout(lbr[layer], lbi[layer], bbr[layer], bbi[layer], ssm_c_re[layer], ssm_c_im[layer])
        conv0 = jnp.zeros((bp, CONV_HIST, c_conv), f32)
        h0 = jnp.zeros((bp, nj, 2 * n_groups * n_state // nj), f32)
        y_p, h_p, state = _run_layer(y_p, h_p, bp, lp, layer, p, big, s5p, conv0, h0, None, g_next,
                                     kv_stack=(depth, k_all, v_all))
        k_all, v_all = state[0], state[1]
        outs_p.append(state)
        past = (cache_k, cache_v, logf_flat, page_table)
        y_s, h_s, state = _run_layer(y_s, h_s, bs, ls, layer, p, big, s5p, state_conv[layer],
                                     _pack_state(state_ssm_re[layer], state_ssm_im[layer], nj), past, g_next)
        outs_s.append(state)
    stack = lambda outs, i: jnp.stack([o[i] for o in outs])
    heads = b_forget.shape[1]
    kv_shape = (depth, bp, lp, heads, k_all.shape[-1] // heads)
    return ((y_p.reshape(bp, lp, d), y_s.reshape(bs, ls, d), k_all.reshape(kv_shape), v_all.reshape(kv_shape))
            + tuple(stack(outs_p, i) for i in range(2, 6))
            + tuple(stack(outs_s, i) for i in range(6)))
```

```python
import functools
import math

import jax
import jax.numpy as jnp
from jax import lax
from jax.experimental import pallas as pl
from jax.experimental.pallas import tpu as pltpu

f32 = jnp.float32
bf16 = jnp.bfloat16
SDS = jax.ShapeDtypeStruct

LANES = 128
SUBLANES = 8
VMEM_LIMIT_BYTES = 56 * 2 ** 20
EPS = 1e-6
NEG_INF = -1e30
CONV_HIST = 32


def _cparams(*sem):
    return pltpu.CompilerParams(dimension_semantics=sem, vmem_limit_bytes=VMEM_LIMIT_BYTES)


def _sigmoid(x):
    return 1.0 / (1.0 + jnp.exp(-x))


def _tile(n, pref):
    if n <= pref:
        return n
    t = pref
    while n % t:
        t //= 2
    return t


def _rms(x, g):
    return x * lax.rsqrt(jnp.mean(x * x, axis=-1, keepdims=True) + EPS) * g


def _rmsnorm_kernel(x_ref, g_ref, o_ref):
    o_ref[...] = _rms(x_ref[...], g_ref[...]).astype(o_ref.dtype)


def _rmsnorm(x, g):
    m, d = x.shape
    tm = _tile(m, 256)
    return pl.pallas_call(
        _rmsnorm_kernel, out_shape=SDS((m, d), bf16), grid=(m // tm,),
        in_specs=[pl.BlockSpec((tm, d), lambda i: (i, 0)), pl.BlockSpec((1, d), lambda i: (0, 0))],
        out_specs=pl.BlockSpec((tm, d), lambda i: (i, 0)),
        compiler_params=_cparams("parallel"), name="rmsnorm")(x, g.reshape(1, d))


NT_DIMS = (((1,), (1,)), ((), ()))
ROW_ALIGN = 16


def _first_row_block(weights_outer):
    return pl.program_id(1 if weights_outer else 0) == 0


def _side_spec(ms, tn, n_blocks, weights_outer):
    if weights_outer:
        return pl.BlockSpec((ms, tn), lambda a, b: (0, a))
    return pl.BlockSpec((ms, tn), lambda a, b: (0, jnp.where(a == 0, b, n_blocks - 1)))


def _mm_kernel(x_ref, w_ref, *rest, n_out, nt, has_side, has_prev, weights_outer):
    xs_ref = rest[0] if has_side else None
    outs = rest[int(has_side) + int(has_prev):]
    o_refs, s_refs = outs[:n_out], outs[n_out:]
    w = (w_ref[0] if nt else w_ref[...]).astype(bf16)

    def mm(a):
        if nt:
            return lax.dot_general(a, w, NT_DIMS, preferred_element_type=f32)
        return jnp.dot(a, w, preferred_element_type=f32)

    acc = mm(x_ref[...])
    for o in o_refs:
        o[...] = acc.astype(o.dtype)
    if has_side:
        @pl.when(_first_row_block(weights_outer))
        def _():
            acc_s = mm(xs_ref[...])
            for o in s_refs:
                o[...] = acc_s.astype(o.dtype)


def _mm(x, w, layer, out_dtypes, *, xs=None, nt=False, off=0, n=None, tm=1024, tn=512, weights_outer=True,
        stack_first=None, name="mm"):
    m, k = x.shape
    n_total = w.shape[1] if nt else w.shape[2]
    n = n_total - off if n is None else n
    tm = _tile(m, tm)
    tn = _tile(n, tn)
    assert n % tn == 0
    if weights_outer:
        grid = (n // tn, m // tm)
        jj = lambda a, b: a
        ii = lambda a, b: b
        xspec = pl.BlockSpec((tm, k), lambda a, b: (b, 0))
    else:
        grid = (m // tm, n // tn)
        jj = lambda a, b: b
        ii = lambda a, b: a
        xspec = pl.BlockSpec((tm, k), lambda a, b: (a, 0), pipeline_mode=pl.Buffered(1))
    if nt:
        assert off % ROW_ALIGN == 0 and tn % ROW_ALIGN == 0
        wspec = pl.BlockSpec((pl.Element(1), pl.Element(tn), pl.Element(k)),
                             lambda a, b: (layer, pl.multiple_of(off + jj(a, b) * tn, ROW_ALIGN), 0))
    else:
        assert off % tn == 0
        wspec = pl.BlockSpec((None, k, tn), lambda a, b: (layer, 0, jj(a, b) + off // tn))
    n_out = len(out_dtypes)
    out_shape = [SDS((m, n), dt) for dt in out_dtypes]
    out_specs = [pl.BlockSpec((tm, tn), lambda a, b: (ii(a, b), jj(a, b))) for _ in out_dtypes]
    operands, in_specs, aliases = [x, w], [xspec, wspec], {}
    if xs is not None:
        ms = xs.shape[0]
        operands.append(xs)
        in_specs.append(pl.BlockSpec((ms, k), lambda a, b: (0, 0)))
        out_shape += [SDS((ms, n), dt) for dt in out_dtypes]
        out_specs += [_side_spec(ms, tn, n // tn, weights_outer) for _ in out_dtypes]
    if stack_first is not None:
        depth, prev = stack_first
        out_shape[0] = SDS((depth, m, n), out_dtypes[0])
        out_specs[0] = pl.BlockSpec((None, tm, tn), lambda a, b: (layer, ii(a, b), jj(a, b)))
        if prev is not None:
            aliases = {len(operands): 0}
            operands.append(prev)
            in_specs.append(pl.BlockSpec(memory_space=pl.ANY))
    outs = pl.pallas_call(
        functools.partial(_mm_kernel, n_out=n_out, nt=nt, has_side=xs is not None,
                          has_prev=bool(aliases), weights_outer=weights_outer),
        out_shape=out_shape, grid=grid, in_specs=in_specs, out_specs=out_specs,
        input_output_aliases=aliases,
        compiler_params=_cparams("arbitrary", "arbitrary"), name=name)(*operands)
    return outs if xs is None else (outs[:n_out], outs[n_out:])


def _conv_kernel(za_ref, zb_ref, prev_ref, w_ref, cb_ref, lg_ref, lb_ref, o_ref, new_ref,
                 ext_ref, conv_ref, *, T, W):
    t = pl.program_id(1)

    nc = cb_ref.shape[1]

    @pl.when(t == 0)
    def _():
        ext_ref[0:CONV_HIST, :] = prev_ref[...]
        ext_ref[CONV_HIST + T:CONV_HIST + T + SUBLANES, :] = jnp.zeros((SUBLANES, nc), f32)

    @pl.when(t > 0)
    def _():
        ext_ref[0:CONV_HIST, :] = ext_ref[T:T + CONV_HIST, :]

    ext_ref[CONV_HIST:CONV_HIST + T, :] = za_ref[...] * _sigmoid(zb_ref[...])
    rc = min(T, 64)
    base = CONV_HIST - (W - 1)
    for r0 in range(0, T, rc):
        acc = jnp.broadcast_to(cb_ref[...], (rc, nc))
        for s in range(SUBLANES):
            part = None
            for k in range(W):
                if (base + k) % SUBLANES != s:
                    continue
                a0 = r0 + base + k - s
                term = w_ref[k:k + 1, :] * ext_ref[a0:a0 + rc + (SUBLANES if s else 0), :]
                part = term if part is None else part + term
            if part is not None:
                acc = acc + part[s:s + rc, :]
        conv_ref[r0:r0 + rc, :] = acc
    c = conv_ref[...]
    mu = jnp.mean(c, axis=-1, keepdims=True)
    xc = c - mu
    y = xc * lax.rsqrt(jnp.mean(xc * xc, axis=-1, keepdims=True) + EPS) * lg_ref[...] + lb_ref[...]
    o_ref[...] = (y * _sigmoid(y)).astype(o_ref.dtype)
    new_ref[...] = ext_ref[T:T + CONV_HIST, :]


def _conv_prompt(zglu, prev32, conv_w, conv_b, ln_g, ln_b, b, l):
    m, c2 = zglu.shape
    c = c2 // 2
    w = conv_w.shape[0]
    t = _tile(l, 256)
    nt = l // t
    kern = functools.partial(_conv_kernel, T=t, W=w)
    row = lambda bi, ti: (bi * nt + ti, 0)
    vec = pl.BlockSpec((1, c), lambda bi, ti: (0, 0))
    return pl.pallas_call(
        kern, out_shape=[SDS((m, c), bf16), SDS((b, CONV_HIST, c), f32)], grid=(b, nt),
        in_specs=[pl.BlockSpec((t, c), row), pl.BlockSpec((t, c), lambda bi, ti: (bi * nt + ti, 1)),
                  pl.BlockSpec((None, CONV_HIST, c), lambda bi, ti: (bi, 0, 0)),
                  pl.BlockSpec((w, c), lambda bi, ti: (0, 0)), vec, vec, vec],
        out_specs=[pl.BlockSpec((t, c), row), pl.BlockSpec((None, CONV_HIST, c), lambda bi, ti: (bi, 0, 0))],
        scratch_shapes=[pltpu.VMEM((t + CONV_HIST + SUBLANES, c), f32), pltpu.VMEM((t, c), f32)],
        compiler_params=_cparams("parallel", "arbitrary"), name="conv_prompt",
    )(zglu, zglu, prev32, conv_w, conv_b.reshape(1, c), ln_g.reshape(1, c), ln_b.reshape(1, c))


def _conv_step_kernel(zg_ref, st_ref, w_ref, cb_ref, lg_ref, lb_ref, o_ref, new_ref, *, W):
    c = cb_ref.shape[1]
    nb = zg_ref.shape[0]
    a = zg_ref[:, 0:c] * _sigmoid(zg_ref[:, c:2 * c])
    rows = []
    for bi in range(nb):
        hist = st_ref[bi]
        conv = (jnp.sum(hist * w_ref[0:W - 1, :], axis=0, keepdims=True)
                + w_ref[W - 1:W, :] * a[bi:bi + 1, :] + cb_ref[...])
        rows.append(conv)
        new_ref[bi, 0:W - 2, :] = st_ref[bi, 1:W - 1, :]
        new_ref[bi, W - 2:W - 1, :] = a[bi:bi + 1, :]
    cv = jnp.concatenate(rows, axis=0)
    mu = jnp.mean(cv, axis=-1, keepdims=True)
    xc = cv - mu
    y = xc * lax.rsqrt(jnp.mean(xc * xc, axis=-1, keepdims=True) + EPS) * lg_ref[...] + lb_ref[...]
    o_ref[...] = (y * _sigmoid(y)).astype(o_ref.dtype)


def _conv_step(zglu, state, conv_w, conv_b, ln_g, ln_b):
    nb, c2 = zglu.shape
    c = c2 // 2
    w = conv_w.shape[0]
    return pl.pallas_call(
        functools.partial(_conv_step_kernel, W=w),
        out_shape=[SDS((nb, c), bf16), SDS((nb, w - 1, c), f32)],
        compiler_params=_cparams(), name="conv_step",
    )(zglu, state, conv_w, conv_b.reshape(1, c), ln_g.reshape(1, c), ln_b.reshape(1, c))


def _log_sigmoid(z):
    return jnp.minimum(z, 0.0) - jnp.log1p(jnp.exp(-jnp.abs(z)))


def _logf_kernel(zf_ref, b_ref, lf_ref, cum_ref, *, H):
    lf = _log_sigmoid(zf_ref[:, 0:H] + b_ref[...])
    lf_ref[...] = lf
    n = lf.shape[0]
    row = lax.broadcasted_iota(jnp.int32, lf.shape, 0)
    x = lf
    s = 1
    while s < n:
        x = x + jnp.where(row >= s, pltpu.roll(x, s, axis=0), 0.0)
        s *= 2
    cum_ref[...] = x


def _logf_prompt(zf, b_forget, b, l):
    h = b_forget.shape[0]
    return pl.pallas_call(
        functools.partial(_logf_kernel, H=h),
        out_shape=[SDS((b, l, h), f32), SDS((b, l, h), f32)], grid=(b,),
        in_specs=[pl.BlockSpec((l, zf.shape[1]), lambda i: (i, 0)), pl.BlockSpec((1, h), lambda i: (0, 0))],
        out_specs=[pl.BlockSpec((None, l, h), lambda i: (i, 0, 0))] * 2,
        compiler_params=_cparams("parallel"), name="logf_prompt")(zf, b_forget.reshape(1, h))


def _logf_step_kernel(zf_ref, b_ref, lf_ref, *, H):
    lf_ref[...] = _log_sigmoid(zf_ref[:, 0:H] + b_ref[...])


def _logf_step(zf, b_forget):
    h = b_forget.shape[0]
    return pl.pallas_call(functools.partial(_logf_step_kernel, H=h), out_shape=SDS((zf.shape[0], h), f32),
                          compiler_params=_cparams(), name="logf_step")(zf, b_forget.reshape(1, h))


def _flash_kernel(q_ref, k_ref, v_ref, cc_ref, cr_ref, o_ref, *, tq, scale, H):
    h = pl.program_id(1)
    n = q_ref.shape[0]
    nt = (((1,), (1,)), ((), ()))
    lane = lax.broadcasted_iota(jnp.int32, (n, H), 1)
    cq = jnp.sum(jnp.where(lane == h, cc_ref[...], 0.0), axis=1, keepdims=True)
    rr = lax.broadcasted_iota(jnp.int32, (tq, tq), 0)
    cc = lax.broadcasted_iota(jnp.int32, (tq, tq), 1)
    for i in range(n // tq):
        lo, hi = i * tq, (i + 1) * tq
        q = q_ref[lo:hi, :]
        cqi = cq[lo:hi, :]
        sd = lax.dot_general(q, k_ref[lo:hi, :], nt, preferred_element_type=f32) * scale
        sd = sd + cqi - cr_ref[:, lo:hi]
        sd = jnp.where(cc <= rr, sd, NEG_INF)
        m = jnp.max(sd, axis=1, keepdims=True)
        if i > 0:
            so = lax.dot_general(q, k_ref[0:lo, :], nt, preferred_element_type=f32) * scale
            so = so + cqi - cr_ref[:, 0:lo]
            m = jnp.maximum(m, jnp.max(so, axis=1, keepdims=True))
            po = jnp.exp(so - m)
            den = jnp.sum(po, axis=1, keepdims=True)
            acc = jnp.dot(po.astype(bf16), v_ref[0:lo, :], preferred_element_type=f32)
        pd = jnp.exp(sd - m)
        if i > 0:
            den = den + jnp.sum(pd, axis=1, keepdims=True)
            acc = acc + jnp.dot(pd.astype(bf16), v_ref[lo:hi, :], preferred_element_type=f32)
        else:
            den = jnp.sum(pd, axis=1, keepdims=True)
            acc = jnp.dot(pd.astype(bf16), v_ref[lo:hi, :], preferred_element_type=f32)
        o_ref[lo:hi, :] = (acc / den).astype(o_ref.dtype)


def _flash_prompt(q, k, v, cum, b, l, h, dh):
    m = q.shape[0]
    tq = _tile(l, 512)
    cum_row = jnp.transpose(cum, (0, 2, 1)).reshape(b, h, 1, l)
    blk = pl.BlockSpec((l, dh), lambda bi, hi: (bi, hi))
    return pl.pallas_call(
        functools.partial(_flash_kernel, tq=tq, scale=dh ** -0.5, H=h),
        out_shape=SDS((m, h * dh), bf16), grid=(b, h),
        in_specs=[blk, blk, blk,
                  pl.BlockSpec((None, l, h), lambda bi, hi: (bi, 0, 0)),
                  pl.BlockSpec((None, None, 1, l), lambda bi, hi: (bi, hi, 0, 0))],
        out_specs=blk,
        compiler_params=_cparams("parallel", "arbitrary"), name="flash_prompt")(q, k, v, cum, cum_row)


PAGE_GROUP = SUBLANES


def _decode_kernel(pt_ref, q_ref, kn_ref, vn_ref, lfn_ref, *rest, scale, NPG, G):
    k_refs, v_refs, f_refs = rest[0:G], rest[G:2 * G], rest[2 * G:3 * G]
    o_ref = rest[3 * G]
    m_s, l_s, acc_s, pref_s = rest[3 * G + 1:]
    bi = pl.program_id(0)
    p = pl.program_id(1)
    r, h, dh = k_refs[0].shape
    n = r * h

    @pl.when(p == 0)
    def _():
        m_s[...] = jnp.full(m_s.shape, NEG_INF, f32)
        l_s[...] = jnp.zeros(l_s.shape, f32)
        acc_s[...] = jnp.zeros(acc_s.shape, f32)
        pref_s[...] = jnp.zeros(pref_s.shape, f32)

    rows = []
    for g in range(G):
        pg = pt_ref[bi * NPG + p * G + g]
        rows.append(f_refs[g][pl.ds(jnp.bitwise_and(pg, PAGE_GROUP - 1), 1), :])
    lf = jnp.concatenate(rows, axis=0) if G > 1 else rows[0]
    lane = lax.broadcasted_iota(jnp.int32, lf.shape, 1)
    cw, tot = lf, lf
    s = h
    while s < n:
        cw = cw + jnp.where(lane >= s, pltpu.roll(cw, s, axis=1), 0.0)
        tot = tot + pltpu.roll(tot, s, axis=1)
        s *= 2
    off = pref_s[...]
    biases = []
    for g in range(G):
        biases.append(off + cw[g:g + 1, :])
        off = off + tot[g:g + 1, :]
    pref_s[...] = off
    bias = jnp.concatenate(biases, axis=1) if G > 1 else biases[0]

    qb = q_ref[...]
    sts =[lax.dot_general(qb, k_refs[g][...].reshape(n, dh).astype(bf16), NT_DIMS,
                           preferred_element_type=f32) for g in range(G)]
    st = (jnp.concatenate(sts, axis=1) if G > 1 else sts[0]) * scale
    s = st - bias
    lane = lax.broadcasted_iota(jnp.int32, s.shape, 1)
    sub = lax.broadcasted_iota(jnp.int32, s.shape, 0)
    s = jnp.where(jnp.bitwise_and(lane, h - 1) == sub, s, NEG_INF)
    m_old = m_s[...]
    m_new = jnp.maximum(m_old, jnp.max(s, axis=1, keepdims=True))
    alpha = jnp.exp(m_old - m_new)
    pr = jnp.exp(s - m_new)
    l_s[...] = alpha * l_s[...] + jnp.sum(pr, axis=1, keepdims=True)
    acc = alpha * acc_s[...]
    for g in range(G):
        acc = acc + jnp.dot(pr[:, g * n:(g + 1) * n].astype(bf16), v_refs[g][...].reshape(n, dh).astype(bf16),
                            preferred_element_type=f32)
    acc_s[...] = acc
    m_s[...] = m_new

    @pl.when(p == NPG // G - 1)
    def _():
        cn_row = pref_s[:, 0:h] + lfn_ref[...]
        eye = lax.broadcasted_iota(jnp.int32, (h, h), 0) == lax.broadcasted_iota(jnp.int32, (h, h), 1)
        cn = jnp.sum(jnp.where(eye, jnp.broadcast_to(cn_row, (h, h)), 0.0), axis=1, keepdims=True)
        s_new = jnp.sum(qb.astype(f32) * kn_ref[...], axis=1, keepdims=True) * scale - cn
        m_o = m_s[...]
        m_f = jnp.maximum(m_o, s_new)
        a = jnp.exp(m_o - m_f)
        pn = jnp.exp(s_new - m_f)
        den = a * l_s[...] + pn
        o_ref[...] = ((a * acc_s[...] + pn * vn_ref[...]) / den).astype(o_ref.dtype)


def _flatten_logf(cache_logf):
    d, n_pool, r, h = cache_logf.shape
    pad = -n_pool % PAGE_GROUP
    flat = cache_logf.reshape(d, n_pool, r * h)
    if pad:
        flat = jnp.pad(flat, ((0, 0), (0, pad), (0, 0)))
    return flat.reshape(d, (n_pool + pad) // PAGE_GROUP, PAGE_GROUP, r * h)


def _decode_attention(q, k_new, v_new, lf_new, cache_k, cache_v, logf_flat, page_table, layer):
    nb, h, dh = q.shape
    _, _, r, _, _ = cache_k.shape
    npg = page_table.shape[1]
    assert h & (h - 1) == 0 and h % SUBLANES == 0 and r & (r - 1) == 0
    g = max(c for c in (8, 4, 2, 1) if npg % c == 0)
    pt = page_table.reshape(-1).astype(jnp.int32)
    page_id = lambda bi, pi, pt_ref, gi: pt_ref[bi * npg + pi * g + gi]
    new = pl.BlockSpec((None, h, dh), lambda bi, pi, pt_ref: (bi, 0, 0))
    page = [pl.BlockSpec((None, None, r, h, dh),
                         functools.partial(lambda bi, pi, pt_ref, gi: (layer, page_id(bi, pi, pt_ref, gi), 0, 0, 0), gi=gi))
            for gi in range(g)]
    flat = [pl.BlockSpec((None, None, PAGE_GROUP, r * h),
                         functools.partial(lambda bi, pi, pt_ref, gi:
                                           (layer, page_id(bi, pi, pt_ref, gi) // PAGE_GROUP, 0, 0), gi=gi))
            for gi in range(g)]
    grid_spec = pltpu.PrefetchScalarGridSpec(
        num_scalar_prefetch=1, grid=(nb, npg // g),
        in_specs=[new, new, new, pl.BlockSpec((None, 1, h), lambda bi, pi, pt_ref: (bi, 0, 0))]
        + page + page + flat,
        out_specs=new,
        scratch_shapes=[pltpu.VMEM((h, 1), f32), pltpu.VMEM((h, 1), f32), pltpu.VMEM((h, dh), f32),
                        pltpu.VMEM((1, r * h), f32)])
    return pl.pallas_call(
        functools.partial(_decode_kernel, scale=dh ** -0.5, NPG=npg, G=g),
        out_shape=SDS((nb, h, dh), bf16), grid_spec=grid_spec,
        compiler_params=_cparams("parallel", "arbitrary"), name="decode_attention",
    )(pt, q, k_new, v_new, lf_new.reshape(nb, 1, h), *([cache_k] * g), *([cache_v] * g), *([logf_flat] * g))


def _disc_kernel(lr_ref, li_ref, ldt_ref, br_ref, bi_ref, lbr_ref, lbi_ref, bbr_ref, bbi_ref, cr_s, ci_s):
    lr = lr_ref[...]
    li = li_ref[...]
    dt = jnp.exp(ldt_ref[...])
    er = jnp.exp(lr * dt)
    ang = li * dt
    lbr = er * jnp.cos(ang)
    lbi = er * jnp.sin(ang)
    lbr_ref[...] = lbr
    lbi_ref[...] = lbi
    nr = lbr - 1.0
    den = lr * lr + li * li
    cr_s[...] = (nr * lr + lbi * li) / den
    ci_s[...] = (lbi * lr - nr * li) / den
    for g in range(lr.shape[0]):
        cr = cr_s[g:g + 1, :]
        ci = ci_s[g:g + 1, :]
        br = br_ref[g]
        bi = bi_ref[g]
        bbr_ref[g] = cr * br - ci * bi
        bbi_ref[g] = cr * bi + ci * br


def _discretize(lam_re, lam_im, log_dt, b_re, b_im):
    d, g, n = lam_re.shape
    p = b_re.shape[-1]
    dg = d * g
    bt = lambda x: jnp.transpose(x, (0, 1, 3, 2)).reshape(dg, p, n)
    lbr, lbi, bbr, bbi = pl.pallas_call(
        _disc_kernel,
        out_shape=[SDS((dg, n), f32), SDS((dg, n), f32), SDS((dg, p, n), f32), SDS((dg, p, n), f32)],
        scratch_shapes=[pltpu.VMEM((dg, n), f32), pltpu.VMEM((dg, n), f32)],
        compiler_params=_cparams(), name="s5_discretize",
    )(lam_re.reshape(dg, n), lam_im.reshape(dg, n), log_dt.reshape(dg, 1), bt(b_re), bt(b_im))
    return (lbr.reshape(d, g, n), lbi.reshape(d, g, n), bbr.reshape(d, g, p, n), bbi.reshape(d, g, p, n))


def _s5_layout(lbr, lbi, bbr, bbi, c_re, c_im):
    g, n = lbr.shape
    p = bbr.shape[1]
    gb = LANES // p
    nj = g // gb
    lam = jnp.concatenate([lbr.reshape(nj, gb * n), lbi.reshape(nj, gb * n)], axis=1)
    eye = jnp.eye(gb, dtype=f32)

    def in_block(bb):
        return jnp.einsum('jgpn,gh->jgphn', bb.reshape(nj, gb, p, n), eye).reshape(nj, gb * p, gb * n)

    def out_block(c):
        return jnp.einsum('jgpn,gh->jhngp', c.reshape(nj, gb, p, n), eye).reshape(nj, gb * n, gb * p)

    wb = jnp.concatenate([in_block(bbr), in_block(bbi)], axis=2).astype(bf16)
    wc = jnp.concatenate([out_block(c_re), -out_block(c_im)], axis=1).astype(bf16)
    return lam, wb, wc


def _gelu_tanh(x):
    return 0.5 * x * (1.0 + jnp.tanh(math.sqrt(2.0 / math.pi) * (x + 0.044715 * (x * x * x))))


def _s5_tail(y, wg_ref, bg_ref):
    yg = _gelu_tanh(y)
    gate = jnp.dot(yg.astype(bf16), wg_ref[...].astype(bf16), preferred_element_type=f32) + bg_ref[...]
    return yg * _sigmoid(gate)


def _s5_kernel(u_ref, h0_ref, lam_ref, wb_ref, wc_ref, d_ref, wg_ref, bg_ref, y_ref, hfin_ref,
               hs_ref, st_ref, *, T, NJ, NC):
    @pl.when(pl.program_id(1) == 0)
    def _():
        st_ref[...] = h0_ref[...]

    u = u_ref[...]
    ub = u.astype(bf16)
    for j in range(NJ):
        r = jnp.dot(ub[:, LANES * j:LANES * (j + 1)], wb_ref[j], preferred_element_type=f32)
        for c in range(NC):
            hs_ref[c, pl.ds(j, T, stride=NJ), :] = r[:, LANES * c:LANES * (c + 1)]
    half = NC // 2
    lam_r = [lam_ref[:, LANES * c:LANES * (c + 1)] for c in range(half)]
    lam_i = [lam_ref[:, LANES * (half + c):LANES * (half + c + 1)] for c in range(half)]

    def body(t, h):
        r0 = pl.multiple_of(t * NJ, NJ)
        new = [None] * NC
        for c in range(half):
            hr, hi = h[c], h[half + c]
            nr = lam_r[c] * hr - lam_i[c] * hi + hs_ref[c, pl.ds(r0, NJ), :]
            ni = lam_r[c] * hi + lam_i[c] * hr + hs_ref[half + c, pl.ds(r0, NJ), :]
            hs_ref[c, pl.ds(r0, NJ), :] = nr
            hs_ref[half + c, pl.ds(r0, NJ), :] = ni
            new[c], new[half + c] = nr, ni
        return tuple(new)

    h = lax.fori_loop(0, T, body, tuple(st_ref[:, LANES * c:LANES * (c + 1)] for c in range(NC)),
                      unroll=8 if T % 8 == 0 else 1)
    for c in range(NC):
        st_ref[:, LANES * c:LANES * (c + 1)] = h[c]
    hfin_ref[...] = st_ref[...]
    ys = []
    for j in range(NJ):
        hj = jnp.concatenate([hs_ref[c, pl.ds(j, T, stride=NJ), :] for c in range(NC)], axis=1)
        ys.append(jnp.dot(hj.astype(bf16), wc_ref[j], preferred_element_type=f32))
    y = jnp.concatenate(ys, axis=1) + d_ref[...] * u
    y_ref[...] = _s5_tail(y, wg_ref, bg_ref).astype(y_ref.dtype)


def _s5_prompt(u, h0, lam, wb, wc, d_skip, w_glu, b_glu, b, l):
    m, c = u.shape
    nj, _, sw = wb.shape
    nc = sw // LANES
    assert nj == SUBLANES and c == nj * LANES
    t = _tile(l, 512)
    nt = l // t
    const2 = lambda shape: pl.BlockSpec(shape, lambda bi, ti: (0, 0))
    const3 = lambda shape: pl.BlockSpec(shape, lambda bi, ti: (0, 0, 0))
    y, hfin = pl.pallas_call(
        functools.partial(_s5_kernel, T=t, NJ=nj, NC=nc),
        out_shape=[SDS((m, c), bf16), SDS((b, nj, sw), f32)], grid=(b, nt),
        in_specs=[pl.BlockSpec((t, c), lambda bi, ti: (bi * nt + ti, 0)),
                  pl.BlockSpec((None, nj, sw), lambda bi, ti: (bi, 0, 0)),
                  const2((nj, sw)), const3((nj, LANES, sw)), const3((nj, sw, LANES)),
                  const2((1, c)), const2((c, c)), const2((1, c))],
        out_specs=[pl.BlockSpec((t, c), lambda bi, ti: (bi * nt + ti, 0)),
                   pl.BlockSpec((None, nj, sw), lambda bi, ti: (bi, 0, 0))],
        scratch_shapes=[pltpu.VMEM((nc, t * nj, LANES), f32), pltpu.VMEM((nj, sw), f32)],
        compiler_params=_cparams("parallel", "arbitrary"), name="s5_prompt",
    )(u, h0, lam, wb, wc, d_skip.reshape(1, c), w_glu, b_glu.reshape(1, c))
    return y, hfin


def _s5_step_kernel(u_ref, h0_ref, lam_ref, wb_ref, wc_ref, d_ref, wg_ref, bg_ref, y_ref, hn_ref, *, NJ):
    u = u_ref[...]
    ub = u.astype(bf16)
    half = lam_ref.shape[1] // 2
    ys = []
    for j in range(NJ):
        r = jnp.dot(ub[:, LANES * j:LANES * (j + 1)], wb_ref[j], preferred_element_type=f32)
        lr, li = lam_ref[j:j + 1, 0:half], lam_ref[j:j + 1, half:]
        hr, hi = h0_ref[j, :, 0:half], h0_ref[j, :, half:]
        nr = lr * hr - li * hi + r[:, 0:half]
        ni = lr * hi + li * hr + r[:, half:]
        hn_ref[j, :, 0:half] = nr
        hn_ref[j, :, half:] = ni
        hj = jnp.concatenate([nr, ni], axis=1).astype(bf16)
        ys.append(jnp.dot(hj, wc_ref[j], preferred_element_type=f32))
    y = jnp.concatenate(ys, axis=1) + d_ref[...] * u
    y_ref[...] = _s5_tail(y, wg_ref, bg_ref).astype(y_ref.dtype)


def _s5_step(u, h0, lam, wb, wc, d_skip, w_glu, b_glu):
    nb, c = u.shape
    nj, _, sw = wb.shape
    return pl.pallas_call(
        functools.partial(_s5_step_kernel, NJ=nj),
        out_shape=[SDS((nb, c), bf16), SDS((nj, nb, sw), f32)],
        compiler_params=_cparams(), name="s5_step",
    )(u, h0, lam, wb, wc, d_skip.reshape(1, c), w_glu, b_glu.reshape(1, c))


def _merge_kernel(wa_ref, wo_ref, wy_ref, a_ref, o_ref, y_ref, g0_ref, g1_ref, g2_ref,
                  as_ref, os_ref, ys_ref, gs0_ref, gs1_ref, gs2_ref, m_ref, ms_ref):
    wa, wo, wy = wa_ref[...].astype(bf16), wo_ref[...].astype(bf16), wy_ref[...].astype(bf16)

    def mix(a, o, y, g0, g1, g2):
        ba = jnp.dot(a[...], wa, preferred_element_type=f32)
        bb = jnp.dot(o[...], wo, preferred_element_type=f32)
        bc = jnp.dot(y[...], wy, preferred_element_type=f32)
        return _sigmoid(g0[...]) * ba + _sigmoid(g1[...]) * bb + _sigmoid(g2[...]) * bc

    m_ref[...] = mix(a_ref, o_ref, y_ref, g0_ref, g1_ref, g2_ref).astype(m_ref.dtype)

    @pl.when(_first_row_block(True))
    def _():
        ms_ref[...] = mix(as_ref, os_ref, ys_ref, gs0_ref, gs1_ref, gs2_ref).astype(ms_ref.dtype)


def _merge(branches, side_branches, w_conv, w_attn, w_ssm, layer, gates, side_gates):
    a, o, y = branches
    m = a.shape[0]
    ms = side_branches[0].shape[0]
    d = w_conv.shape[2]
    tm = _tile(m, 512)
    tn = _tile(d, 512)
    nb = d // tn
    lhs = lambda x: pl.BlockSpec((tm, x.shape[1]), lambda j, i: (i, 0))
    side = lambda x: pl.BlockSpec((ms, x.shape[1]), lambda j, i: (0, 0))
    wsp = lambda w: pl.BlockSpec((None, w.shape[1], tn), lambda j, i: (layer, 0, j))
    gate = lambda gi: pl.BlockSpec((tm, tn), lambda j, i: (i, j + gi * nb))
    sgate = lambda gi: pl.BlockSpec((ms, tn), lambda j, i: (0, j + gi * nb))
    return pl.pallas_call(
        _merge_kernel, out_shape=[SDS((m, d), bf16), SDS((ms, d), bf16)], grid=(nb, m // tm),
        in_specs=[wsp(w_conv), wsp(w_attn), wsp(w_ssm), lhs(a), lhs(o), lhs(y), gate(0), gate(1), gate(2)]
        + [side(x) for x in side_branches] + [sgate(0), sgate(1), sgate(2)],
        out_specs=[pl.BlockSpec((tm, tn), lambda j, i: (i, j)), _side_spec(ms, tn, nb, True)],
        compiler_params=_cparams("arbitrary", "arbitrary"), name="branch_merge",
    )(w_conv, w_attn, w_ssm, a, o, y, gates, gates, gates, *side_branches, side_gates, side_gates, side_gates)


def _resnorm_kernel(x_ref, t_ref, g_ref, g2_ref, xo_ref, h_ref):
    xn = x_ref[...] + _rms(t_ref[...], g_ref[...])
    xo_ref[...] = xn
    h_ref[...] = _rms(xn, g2_ref[...]).astype(h_ref.dtype)


def _resnorm_last_kernel(x_ref, t_ref, g_ref, xo_ref):
    xo_ref[...] = x_ref[...] + _rms(t_ref[...], g_ref[...])


def _resnorm(x, t, g, g_next=None):
    m, d = x.shape
    tm = _tile(m, 256)
    row = pl.BlockSpec((tm, d), lambda i: (i, 0))
    vec = pl.BlockSpec((1, d), lambda i: (0, 0))
    if g_next is None:
        return pl.pallas_call(
            _resnorm_last_kernel, out_shape=SDS((m, d), f32), grid=(m // tm,),
            in_specs=[row, row, vec], out_specs=row,
            compiler_params=_cparams("parallel"), name="resnorm_last")(x, t, g.reshape(1, d)), None
    return pl.pallas_call(
        _resnorm_kernel, out_shape=[SDS((m, d), f32), SDS((m, d), bf16)], grid=(m // tm,),
        in_specs=[row, row, vec, vec], out_specs=[row, row],
        compiler_params=_cparams("parallel"), name="resnorm")(x, t, g.reshape(1, d), g_next.reshape(1, d))


def _ffn_gu_kernel(h_ref, hs_ref, wg_ref, wu_ref, o_ref, os_ref):
    wg, wu = wg_ref[...].astype(bf16), wu_ref[...].astype(bf16)

    def swiglu(h):
        g = jnp.dot(h, wg, preferred_element_type=f32)
        u = jnp.dot(h, wu, preferred_element_type=f32)
        return g * _sigmoid(g) * u

    o_ref[...] = swiglu(h_ref[...]).astype(o_ref.dtype)

    @pl.when(_first_row_block(False))
    def _():
        os_ref[...] = swiglu(hs_ref[...]).astype(os_ref.dtype)


def _ffn_gate_up(h, hs, w_gate, w_up, layer):
    m, d = h.shape
    ms = hs.shape[0]
    f = w_gate.shape[2]
    tm = _tile(m, 2048)
    tn = _tile(f, 256)
    wsp = pl.BlockSpec((None, d, tn), lambda i, j: (layer, 0, j))
    return pl.pallas_call(
        _ffn_gu_kernel, out_shape=[SDS((m, f), bf16), SDS((ms, f), bf16)], grid=(m // tm, f // tn),
        in_specs=[pl.BlockSpec((tm, d), lambda i, j: (i, 0), pipeline_mode=pl.Buffered(1)),
                  pl.BlockSpec((ms, d), lambda i, j: (0, 0)), wsp, wsp],
        out_specs=[pl.BlockSpec((tm, tn), lambda i, j: (i, j)), _side_spec(ms, tn, f // tn, False)],
        compiler_params=_cparams("arbitrary", "arbitrary"), name="ffn_gate_up")(h, hs, w_gate, w_up)


def _run_layer(x, h, xs, hs, b, l, layer, p, big, s5p, conv_state, h0_s, past, g_next, kv_stack):
    bs = xs.shape[0]
    heads = p['b_forget'].shape[0]
    c_conv = p['conv_b'].shape[0]
    d_attn = big['w_branch_attn'].shape[1]
    dh = d_attn // heads
    c_ssm = p['w_glu'].shape[0]
    off_q = 2 * c_conv
    off_k = off_q + d_attn
    off_v = off_k + d_attn
    off_f = off_v + d_attn
    off_u = off_f + heads
    off_g = off_u + c_ssm
    w_in_t = big['w_in_t']

    depth, k_prev, v_prev = kv_stack
    proj = functools.partial(_mm, h, w_in_t, layer, xs=hs, nt=True)
    (zglu,), (zglu_s,) = proj([f32], off=0, n=off_q, name="in_glu")
    (q,), (q_s,) = proj([bf16], off=off_q, n=d_attn, name="in_q")
    (k_all, kb), (k_s, _) = proj([f32, bf16], off=off_k, n=d_attn, stack_first=(depth, k_prev), name="in_k")
    (v_all, vb), (v_s, _) = proj([f32, bf16], off=off_v, n=d_attn, stack_first=(depth, v_prev), name="in_v")
    (zf,), (zf_s,) = proj([f32], off=off_f, n=heads, name="in_f")
    (u,), (u_s,) = proj([f32], off=off_u, n=c_ssm, name="in_u")
    (gates,), (gates_s,) = proj([f32], off=off_g, tm=2048, weights_outer=False, name="in_gates")

    lam, wb, wc = s5p
    conv_args = (p['conv_w'], p['conv_b'], p['ln_conv_g'], p['ln_conv_b'])
    s5_args = (lam, wb, wc, p['ssm_d'].reshape(-1), p['w_glu'], p['b_glu'])
    n_groups, n_state = p['ssm_lambda_re'].shape
    nj = wb.shape[0]

    def unpack(hfin, nb):
        half = hfin.shape[2] // 2
        return hfin[:, :, :half].reshape(nb, n_groups, n_state), hfin[:, :, half:].reshape(nb, n_groups, n_state)

    conv0 = jnp.zeros((b, CONV_HIST, c_conv), f32)
    h0 = jnp.zeros((b, nj, 2 * n_groups * n_state // nj), f32)
    a, conv_new = _conv_prompt(zglu, conv0, *conv_args, b, l)
    conv_new = conv_new[:, CONV_HIST - (p['conv_w'].shape[0] - 1):, :]
    logf, cum = _logf_prompt(zf, p['b_forget'], b, l)
    o = _flash_prompt(q, kb, vb, cum, b, l, heads, dh)
    y, hfin = _s5_prompt(u, h0, *s5_args, b, l)
    state_p = (k_all, v_all, logf.reshape(b, l, heads), conv_new) + unpack(hfin, b)

    a_s, conv_new_s = _conv_step(zglu_s, conv_state, *conv_args)
    logf_s = _logf_step(zf_s, p['b_forget'])
    cache_k, cache_v, logf_flat, page_table = past
    o_s = _decode_attention(q_s.reshape(bs, heads, dh), k_s.reshape(bs, heads, dh), v_s.reshape(bs, heads, dh),
                            logf_s, cache_k, cache_v, logf_flat, page_table, layer).reshape(bs, d_attn)
    y_s, hn = _s5_step(u_s, jnp.transpose(h0_s, (1, 0, 2)), *s5_args)
    state_s = (k_s.reshape(bs, 1, heads, dh), v_s.reshape(bs, 1, heads, dh), logf_s.reshape(bs, 1, heads),
               conv_new_s) + unpack(jnp.transpose(hn, (1, 0, 2)), bs)

    mix, mix_s = _merge((a, o, y), (a_s, o_s, y_s), big['w_branch_conv'], big['w_branch_attn'],
                        big['w_branch_ssm'], layer, gates, gates_s)
    (t,), (t_s,) = _mm(mix, big['w_out'], layer, [f32], xs=mix_s, tm=2048, weights_outer=False, name="w_out")
    x1, hf = _resnorm(x, t, p['g_mix_post'], p['g_ffn_pre'])
    x1_s, hf_s = _resnorm(xs, t_s, p['g_mix_post'], p['g_ffn_pre'])
    act, act_s = _ffn_gate_up(hf, hf_s, big['w_ffn_gate'], big['w_ffn_up'], layer)
    (f,), (f_s,) = _mm(act, big['w_ffn_down'], layer, [f32], xs=act_s, tm=1024, tn=256, weights_outer=False,
                       name="ffn_down")
    x2, h_next = _resnorm(x1, f, p['g_ffn_post'], g_next)
    x2_s, h_next_s = _resnorm(x1_s, f_s, p['g_ffn_post'], g_next)
    return (x2, h_next, state_p), (x2_s, h_next_s, state_s)


def _pack_state(h_re, h_im, nj):
    b = h_re.shape[0]
    return jnp.concatenate([h_re.reshape(b, nj, -1), h_im.reshape(b, nj, -1)], axis=2)


def kernel(x_prompt, x_sample, cache_k, cache_v, cache_logf, state_conv, state_ssm_re, state_ssm_im, page_table, g_mix_pre, w_in, b_forget, conv_w, conv_b, ln_conv_g, ln_conv_b, ssm_lambda_re, ssm_lambda_im, ssm_log_dt, ssm_b_re, ssm_b_im, ssm_c_re, ssm_c_im, ssm_d, w_glu, b_glu, w_branch_conv, w_branch_attn, w_branch_ssm, w_out, g_mix_post, g_ffn_pre, w_ffn_gate, w_ffn_up, w_ffn_down, g_ffn_post):
    small = dict(g_mix_pre=g_mix_pre, b_forget=b_forget, conv_w=conv_w, conv_b=conv_b,
                 ln_conv_g=ln_conv_g, ln_conv_b=ln_conv_b, ssm_lambda_re=ssm_lambda_re, ssm_d=ssm_d,
                 w_glu=w_glu, b_glu=b_glu, g_mix_post=g_mix_post, g_ffn_pre=g_ffn_pre, g_ffn_post=g_ffn_post)
    big = dict(w_in_t=jnp.swapaxes(w_in, 1, 2), w_branch_conv=w_branch_conv, w_branch_attn=w_branch_attn,
               w_branch_ssm=w_branch_ssm, w_out=w_out, w_ffn_gate=w_ffn_gate, w_ffn_up=w_ffn_up,
               w_ffn_down=w_ffn_down.astype(bf16))
    depth = w_in.shape[0]
    bp, lp, d = x_prompt.shape
    bs, ls, _ = x_sample.shape
    assert ls == 1
    n_groups, n_state = ssm_lambda_re.shape[1:]
    p_ssm = ssm_b_re.shape[-1]
    nj = n_groups * p_ssm // LANES

    lbr, lbi, bbr, bbi = _discretize(ssm_lambda_re, ssm_lambda_im, ssm_log_dt, ssm_b_re, ssm_b_im)
    logf_flat = _flatten_logf(cache_logf)

    y_p = x_prompt.reshape(bp * lp, d)
    y_s = x_sample.reshape(bs * ls, d)
    h_p = _rmsnorm(y_p, g_mix_pre[0])
    h_s = _rmsnorm(y_s, g_mix_pre[0])
    outs_p, outs_s = [], []
    k_all = v_all = None
    for layer in range(depth):
        p = {name: val[layer] for name, val in small.items()}
        g_next = g_mix_pre[layer + 1] if layer + 1 < depth else None
        s5p = _s5_layout(lbr[layer], lbi[layer], bbr[layer], bbi[layer], ssm_c_re[layer], ssm_c_im[layer])
        past = (cache_k, cache_v, logf_flat, page_table)
        (y_p, h_p, state_p), (y_s, h_s, state_s) = _run_layer(
            y_p, h_p, y_s, h_s, bp, lp, layer, p, big, s5p, state_conv[layer],
            _pack_state(state_ssm_re[layer], state_ssm_im[layer], nj), past, g_next, (depth, k_all, v_all))
        k_all, v_all = state_p[0], state_p[1]
        outs_p.append(state_p)
        outs_s.append(state_s)
    stack = lambda outs, i: jnp.stack([o[i] for o in outs])
    heads = b_forget.shape[1]
    kv_shape = (depth, bp, lp, heads, k_all.shape[-1] // heads)
    return ((y_p.reshape(bp, lp, d), y_s.reshape(bs, ls, d), k_all.reshape(kv_shape), v_all.reshape(kv_shape))
            + tuple(stack(outs_p, i) for i in range(2, 6))
            + tuple(stack(outs_s, i) for i in range(6)))
```

```python
import functools
import math

import jax
import jax.numpy as jnp
from jax import lax
from jax.experimental import pallas as pl
from jax.experimental.pallas import tpu as pltpu

f32 = jnp.float32
bf16 = jnp.bfloat16
SDS = jax.ShapeDtypeStruct

LANES = 128
SUBLANES = 8
VMEM_LIMIT_BYTES = 56 * 2 ** 20
EPS = 1e-6
NEG_INF = -1e30
LOG2E = math.log2(math.e)
CONV_HIST = 32


def _cparams(*sem):
    return pltpu.CompilerParams(dimension_semantics=sem, vmem_limit_bytes=VMEM_LIMIT_BYTES)


def _sigmoid(x):
    return 1.0 / (1.0 + jnp.exp(-x))


def _tile(n, pref):
    if n <= pref:
        return n
    t = pref
    while n % t:
        t //= 2
    return t


def _rms(x, g):
    return x * lax.rsqrt(jnp.mean(x * x, axis=-1, keepdims=True) + EPS) * g


def _rmsnorm_kernel(x_ref, g_ref, o_ref):
    o_ref[...] = _rms(x_ref[...], g_ref[...]).astype(o_ref.dtype)


def _rmsnorm(x, g):
    m, d = x.shape
    tm = _tile(m, 256)
    return pl.pallas_call(
        _rmsnorm_kernel, out_shape=SDS((m, d), bf16), grid=(m // tm,),
        in_specs=[pl.BlockSpec((tm, d), lambda i: (i, 0)), pl.BlockSpec((1, d), lambda i: (0, 0))],
        out_specs=pl.BlockSpec((tm, d), lambda i: (i, 0)),
        compiler_params=_cparams("parallel"), name="rmsnorm")(x, g.reshape(1, d))


NT_DIMS = (((1,), (1,)), ((), ()))
ROW_ALIGN = 16


def _first_row_block(weights_outer):
    return pl.program_id(1 if weights_outer else 0) == 0


def _side_spec(ms, tn, n_blocks, weights_outer):
    if weights_outer:
        return pl.BlockSpec((ms, tn), lambda a, b: (0, a))
    return pl.BlockSpec((ms, tn), lambda a, b: (0, jnp.where(a == 0, b, n_blocks - 1)))


def _mm_kernel(x_ref, w_ref, *rest, n_out, nt, has_side, has_prev, weights_outer):
    xs_ref = rest[0] if has_side else None
    outs = rest[int(has_side) + int(has_prev):]
    o_refs, s_refs = outs[:n_out], outs[n_out:]
    w = (w_ref[0] if nt else w_ref[...]).astype(bf16)

    def mm(a):
        if nt:
            return lax.dot_general(a, w, NT_DIMS, preferred_element_type=f32)
        return jnp.dot(a, w, preferred_element_type=f32)

    acc = mm(x_ref[...])
    for o in o_refs:
        o[...] = acc.astype(o.dtype)
    if has_side:
        @pl.when(_first_row_block(weights_outer))
        def _():
            acc_s = mm(xs_ref[...])
            for o in s_refs:
                o[...] = acc_s.astype(o.dtype)


def _mm(x, w, layer, out_dtypes, *, xs=None, nt=False, off=0, n=None, tm=1024, tn=512, weights_outer=True,
        stack_first=None, name="mm"):
    m, k = x.shape
    n_total = w.shape[1] if nt else w.shape[2]
    n = n_total - off if n is None else n
    tm = _tile(m, tm)
    tn = _tile(n, tn)
    assert n % tn == 0
    if weights_outer:
        grid = (n // tn, m // tm)
        jj = lambda a, b: a
        ii = lambda a, b: b
        xspec = pl.BlockSpec((tm, k), lambda a, b: (b, 0))
    else:
        grid = (m // tm, n // tn)
        jj = lambda a, b: b
        ii = lambda a, b: a
        xspec = pl.BlockSpec((tm, k), lambda a, b: (a, 0), pipeline_mode=pl.Buffered(1))
    if nt:
        assert off % ROW_ALIGN == 0 and tn % ROW_ALIGN == 0
        wspec = pl.BlockSpec((pl.Element(1), pl.Element(tn), pl.Element(k)),
                             lambda a, b: (layer, pl.multiple_of(off + jj(a, b) * tn, ROW_ALIGN), 0))
    else:
        assert off % tn == 0
        wspec = pl.BlockSpec((None, k, tn), lambda a, b: (layer, 0, jj(a, b) + off // tn))
    n_out = len(out_dtypes)
    out_shape = [SDS((m, n), dt) for dt in out_dtypes]
    out_specs = [pl.BlockSpec((tm, tn), lambda a, b: (ii(a, b), jj(a, b))) for _ in out_dtypes]
    operands, in_specs, aliases = [x, w], [xspec, wspec], {}
    if xs is not None:
        ms = xs.shape[0]
        operands.append(xs)
        in_specs.append(pl.BlockSpec((ms, k), lambda a, b: (0, 0)))
        out_shape += [SDS((ms, n), dt) for dt in out_dtypes]
        out_specs += [_side_spec(ms, tn, n // tn, weights_outer) for _ in out_dtypes]
    if stack_first is not None:
        depth, prev = stack_first
        out_shape[0] = SDS((depth, m, n), out_dtypes[0])
        out_specs[0] = pl.BlockSpec((None, tm, tn), lambda a, b: (layer, ii(a, b), jj(a, b)))
        if prev is not None:
            aliases = {len(operands): 0}
            operands.append(prev)
            in_specs.append(pl.BlockSpec(memory_space=pl.ANY))
    outs = pl.pallas_call(
        functools.partial(_mm_kernel, n_out=n_out, nt=nt, has_side=xs is not None,
                          has_prev=bool(aliases), weights_outer=weights_outer),
        out_shape=out_shape, grid=grid, in_specs=in_specs, out_specs=out_specs,
        input_output_aliases=aliases,
        compiler_params=_cparams("arbitrary", "arbitrary"), name=name)(*operands)
    return outs if xs is None else (outs[:n_out], outs[n_out:])


def _conv_kernel(za_ref, zb_ref, prev_ref, w_ref, cb_ref, lg_ref, lb_ref, o_ref, new_ref,
                 ext_ref, conv_ref, *, T, W):
    t = pl.program_id(1)

    nc = cb_ref.shape[1]

    @pl.when(t == 0)
    def _():
        ext_ref[0:CONV_HIST, :] = prev_ref[...]
        ext_ref[CONV_HIST + T:CONV_HIST + T + SUBLANES, :] = jnp.zeros((SUBLANES, nc), f32)

    @pl.when(t > 0)
    def _():
        ext_ref[0:CONV_HIST, :] = ext_ref[T:T + CONV_HIST, :]

    ext_ref[CONV_HIST:CONV_HIST + T, :] = za_ref[...] * _sigmoid(zb_ref[...])
    rc = min(T, 64)
    base = CONV_HIST - (W - 1)
    for r0 in range(0, T, rc):
        acc = jnp.broadcast_to(cb_ref[...], (rc, nc))
        for s in range(SUBLANES):
            part = None
            for k in range(W):
                if (base + k) % SUBLANES != s:
                    continue
                a0 = r0 + base + k - s
                term = w_ref[k:k + 1, :] * ext_ref[a0:a0 + rc + (SUBLANES if s else 0), :]
                part = term if part is None else part + term
            if part is not None:
                acc = acc + part[s:s + rc, :]
        conv_ref[r0:r0 + rc, :] = acc
    c = conv_ref[...]
    mu = jnp.mean(c, axis=-1, keepdims=True)
    xc = c - mu
    y = xc * lax.rsqrt(jnp.mean(xc * xc, axis=-1, keepdims=True) + EPS) * lg_ref[...] + lb_ref[...]
    o_ref[...] = (y * _sigmoid(y)).astype(o_ref.dtype)
    new_ref[...] = ext_ref[T:T + CONV_HIST, :]


def _conv_prompt(zglu, prev32, conv_w, conv_b, ln_g, ln_b, b, l):
    m, c2 = zglu.shape
    c = c2 // 2
    w = conv_w.shape[0]
    t = _tile(l, 256)
    nt = l // t
    kern = functools.partial(_conv_kernel, T=t, W=w)
    row = lambda bi, ti: (bi * nt + ti, 0)
    vec = pl.BlockSpec((1, c), lambda bi, ti: (0, 0))
    return pl.pallas_call(
        kern, out_shape=[SDS((m, c), bf16), SDS((b, CONV_HIST, c), f32)], grid=(b, nt),
        in_specs=[pl.BlockSpec((t, c), row), pl.BlockSpec((t, c), lambda bi, ti: (bi * nt + ti, 1)),
                  pl.BlockSpec((None, CONV_HIST, c), lambda bi, ti: (bi, 0, 0)),
                  pl.BlockSpec((w, c), lambda bi, ti: (0, 0)), vec, vec, vec],
        out_specs=[pl.BlockSpec((t, c), row), pl.BlockSpec((None, CONV_HIST, c), lambda bi, ti: (bi, 0, 0))],
        scratch_shapes=[pltpu.VMEM((t + CONV_HIST + SUBLANES, c), f32), pltpu.VMEM((t, c), f32)],
        compiler_params=_cparams("parallel", "arbitrary"), name="conv_prompt",
    )(zglu, zglu, prev32, conv_w, conv_b.reshape(1, c), ln_g.reshape(1, c), ln_b.reshape(1, c))


def _conv_step_kernel(zg_ref, st_ref, w_ref, cb_ref, lg_ref, lb_ref, o_ref, new_ref, *, W):
    c = cb_ref.shape[1]
    nb = zg_ref.shape[0]
    a = zg_ref[:, 0:c] * _sigmoid(zg_ref[:, c:2 * c])
    rows = []
    for bi in range(nb):
        hist = st_ref[bi]
        conv = (jnp.sum(hist * w_ref[0:W - 1, :], axis=0, keepdims=True)
                + w_ref[W - 1:W, :] * a[bi:bi + 1, :] + cb_ref[...])
        rows.append(conv)
        new_ref[bi, 0:W - 2, :] = st_ref[bi, 1:W - 1, :]
        new_ref[bi, W - 2:W - 1, :] = a[bi:bi + 1, :]
    cv = jnp.concatenate(rows, axis=0)
    mu = jnp.mean(cv, axis=-1, keepdims=True)
    xc = cv - mu
    y = xc * lax.rsqrt(jnp.mean(xc * xc, axis=-1, keepdims=True) + EPS) * lg_ref[...] + lb_ref[...]
    o_ref[...] = (y * _sigmoid(y)).astype(o_ref.dtype)


def _conv_step(zglu, state, conv_w, conv_b, ln_g, ln_b):
    nb, c2 = zglu.shape
    c = c2 // 2
    w = conv_w.shape[0]
    return pl.pallas_call(
        functools.partial(_conv_step_kernel, W=w),
        out_shape=[SDS((nb, c), bf16), SDS((nb, w - 1, c), f32)],
        compiler_params=_cparams(), name="conv_step",
    )(zglu, state, conv_w, conv_b.reshape(1, c), ln_g.reshape(1, c), ln_b.reshape(1, c))


def _log_sigmoid(z):
    return jnp.minimum(z, 0.0) - jnp.log1p(jnp.exp(-jnp.abs(z)))


def _logf_kernel(zf_ref, b_ref, lf_ref, cum_ref, *, H):
    lf = _log_sigmoid(zf_ref[:, 0:H] + b_ref[...])
    lf_ref[...] = lf
    n = lf.shape[0]
    row = lax.broadcasted_iota(jnp.int32, lf.shape, 0)
    x = lf
    s = 1
    while s < n:
        x = x + jnp.where(row >= s, pltpu.roll(x, s, axis=0), 0.0)
        s *= 2
    cum_ref[...] = x


def _logf_prompt(zf, b_forget, b, l):
    h = b_forget.shape[0]
    return pl.pallas_call(
        functools.partial(_logf_kernel, H=h),
        out_shape=[SDS((b, l, h), f32), SDS((b, l, h), f32)], grid=(b,),
        in_specs=[pl.BlockSpec((l, zf.shape[1]), lambda i: (i, 0)), pl.BlockSpec((1, h), lambda i: (0, 0))],
        out_specs=[pl.BlockSpec((None, l, h), lambda i: (i, 0, 0))] * 2,
        compiler_params=_cparams("parallel"), name="logf_prompt")(zf, b_forget.reshape(1, h))


def _logf_step_kernel(zf_ref, b_ref, lf_ref, *, H):
    lf_ref[...] = _log_sigmoid(zf_ref[:, 0:H] + b_ref[...])


def _logf_step(zf, b_forget):
    h = b_forget.shape[0]
    return pl.pallas_call(functools.partial(_logf_step_kernel, H=h), out_shape=SDS((zf.shape[0], h), f32),
                          compiler_params=_cparams(), name="logf_step")(zf, b_forget.reshape(1, h))


def _flash_kernel(q_ref, k_ref, v_ref, cc_ref, cr_ref, o_ref, *, tq, scale, H):
    h = pl.program_id(1)
    n = q_ref.shape[0]
    nt = (((1,), (1,)), ((), ()))
    lane = lax.broadcasted_iota(jnp.int32, (n, H), 1)
    cq = jnp.sum(jnp.where(lane == h, cc_ref[...], 0.0), axis=1, keepdims=True) * LOG2E
    cr = cr_ref[...] * LOG2E
    scale2 = scale * LOG2E
    rr = lax.broadcasted_iota(jnp.int32, (tq, tq), 0)
    cc = lax.broadcasted_iota(jnp.int32, (tq, tq), 1)
    for i in range(n // tq):
        lo, hi = i * tq, (i + 1) * tq
        q = q_ref[lo:hi, :]
        cqi = cq[lo:hi, :]
        sd = lax.dot_general(q, k_ref[lo:hi, :], nt, preferred_element_type=f32) * scale2
        sd = sd + cqi - cr[:, lo:hi]
        sd = jnp.where(cc <= rr, sd, NEG_INF)
        m = jnp.max(sd, axis=1, keepdims=True)
        if i > 0:
            so = lax.dot_general(q, k_ref[0:lo, :], nt, preferred_element_type=f32) * scale2
            so = so + cqi - cr[:, 0:lo]
            m = jnp.maximum(m, jnp.max(so, axis=1, keepdims=True))
            po = jnp.exp2(so - m)
            den = jnp.sum(po, axis=1, keepdims=True)
            acc = jnp.dot(po.astype(bf16), v_ref[0:lo, :], preferred_element_type=f32)
        pd = jnp.exp2(sd - m)
        if i > 0:
            den = den + jnp.sum(pd, axis=1, keepdims=True)
            acc = acc + jnp.dot(pd.astype(bf16), v_ref[lo:hi, :], preferred_element_type=f32)
        else:
            den = jnp.sum(pd, axis=1, keepdims=True)
            acc = jnp.dot(pd.astype(bf16), v_ref[lo:hi, :], preferred_element_type=f32)
        o_ref[lo:hi, :] = (acc / den).astype(o_ref.dtype)


def _flash_prompt(q, k, v, cum, b, l, h, dh):
    m = q.shape[0]
    tq = _tile(l, 256)
    cum_row = jnp.transpose(cum, (0, 2, 1)).reshape(b, h, 1, l)
    blk = pl.BlockSpec((l, dh), lambda bi, hi: (bi, hi))
    return pl.pallas_call(
        functools.partial(_flash_kernel, tq=tq, scale=dh ** -0.5, H=h),
        out_shape=SDS((m, h * dh), bf16), grid=(b, h),
        in_specs=[blk, blk, blk,
                  pl.BlockSpec((None, l, h), lambda bi, hi: (bi, 0, 0)),
                  pl.BlockSpec((None, None, 1, l), lambda bi, hi: (bi, hi, 0, 0))],
        out_specs=blk,
        compiler_params=_cparams("parallel", "arbitrary"), name="flash_prompt")(q, k, v, cum, cum_row)


PAGE_GROUP = SUBLANES


def _decode_kernel(pt_ref, q_ref, kn_ref, vn_ref, lfn_ref, *rest, scale, NPG, G):
    k_refs, v_refs, f_refs = rest[0:G], rest[G:2 * G], rest[2 * G:3 * G]
    o_ref = rest[3 * G]
    m_s, l_s, acc_s, pref_s = rest[3 * G + 1:]
    bi = pl.program_id(0)
    p = pl.program_id(1)
    r, h, dh = k_refs[0].shape
    n = r * h

    @pl.when(p == 0)
    def _():
        m_s[...] = jnp.full(m_s.shape, NEG_INF, f32)
        l_s[...] = jnp.zeros(l_s.shape, f32)
        acc_s[...] = jnp.zeros(acc_s.shape, f32)
        pref_s[...] = jnp.zeros(pref_s.shape, f32)

    rows = []
    for g in range(G):
        pg = pt_ref[bi * NPG + p * G + g]
        rows.append(f_refs[g][pl.ds(jnp.bitwise_and(pg, PAGE_GROUP - 1), 1), :])
    lf = jnp.concatenate(rows, axis=0) if G > 1 else rows[0]
    lane = lax.broadcasted_iota(jnp.int32, lf.shape, 1)
    cw, tot = lf, lf
    s = h
    while s < n:
        cw = cw + jnp.where(lane >= s, pltpu.roll(cw, s, axis=1), 0.0)
        tot = tot + pltpu.roll(tot, s, axis=1)
        s *= 2
    off = pref_s[...]
    biases = []
    for g in range(G):
        biases.append(off + cw[g:g + 1, :])
        off = off + tot[g:g + 1, :]
    pref_s[...] = off
    bias = jnp.concatenate(biases, axis=1) if G > 1 else biases[0]

    qb = q_ref[...]
    sts =[lax.dot_general(qb, k_refs[g][...].reshape(n, dh).astype(bf16), NT_DIMS,
                           preferred_element_type=f32) for g in range(G)]
    st = (jnp.concatenate(sts, axis=1) if G > 1 else sts[0]) * scale
    s = st - bias
    lane = lax.broadcasted_iota(jnp.int32, s.shape, 1)
    sub = lax.broadcasted_iota(jnp.int32, s.shape, 0)
    s = jnp.where(jnp.bitwise_and(lane, h - 1) == sub, s, NEG_INF)
    m_old = m_s[...]
    m_new = jnp.maximum(m_old, jnp.max(s, axis=1, keepdims=True))
    alpha = jnp.exp(m_old - m_new)
    pr = jnp.exp(s - m_new)
    l_s[...] = alpha * l_s[...] + jnp.sum(pr, axis=1, keepdims=True)
    acc = alpha * acc_s[...]
    for g in range(G):
        acc = acc + jnp.dot(pr[:, g * n:(g + 1) * n].astype(bf16), v_refs[g][...].reshape(n, dh).astype(bf16),
                            preferred_element_type=f32)
    acc_s[...] = acc
    m_s[...] = m_new

    @pl.when(p == NPG // G - 1)
    def _():
        cn_row = pref_s[:, 0:h] + lfn_ref[...]
        eye = lax.broadcasted_iota(jnp.int32, (h, h), 0) == lax.broadcasted_iota(jnp.int32, (h, h), 1)
        cn = jnp.sum(jnp.where(eye, jnp.broadcast_to(cn_row, (h, h)), 0.0), axis=1, keepdims=True)
        s_new = jnp.sum(qb.astype(f32) * kn_ref[...], axis=1, keepdims=True) * scale - cn
        m_o = m_s[...]
        m_f = jnp.maximum(m_o, s_new)
        a = jnp.exp(m_o - m_f)
        pn = jnp.exp(s_new - m_f)
        den = a * l_s[...] + pn
        o_ref[...] = ((a * acc_s[...] + pn * vn_ref[...]) / den).astype(o_ref.dtype)


def _flatten_logf(cache_logf):
    d, n_pool, r, h = cache_logf.shape
    pad = -n_pool % PAGE_GROUP
    flat = cache_logf.reshape(d, n_pool, r * h)
    if pad:
        flat = jnp.pad(flat, ((0, 0), (0, pad), (0, 0)))
    return flat.reshape(d, (n_pool + pad) // PAGE_GROUP, PAGE_GROUP, r * h)


def _decode_attention(q, k_new, v_new, lf_new, cache_k, cache_v, logf_flat, page_table, layer):
    nb, h, dh = q.shape
    _, _, r, _, _ = cache_k.shape
    npg = page_table.shape[1]
    assert h & (h - 1) == 0 and h % SUBLANES == 0 and r & (r - 1) == 0
    g = max(c for c in (8, 4, 2, 1) if npg % c == 0)
    pt = page_table.reshape(-1).astype(jnp.int32)
    page_id = lambda bi, pi, pt_ref, gi: pt_ref[bi * npg + pi * g + gi]
    new = pl.BlockSpec((None, h, dh), lambda bi, pi, pt_ref: (bi, 0, 0))
    page = [pl.BlockSpec((None, None, r, h, dh),
                         functools.partial(lambda bi, pi, pt_ref, gi: (layer, page_id(bi, pi, pt_ref, gi), 0, 0, 0), gi=gi))
            for gi in range(g)]
    flat = [pl.BlockSpec((None, None, PAGE_GROUP, r * h),
                         functools.partial(lambda bi, pi, pt_ref, gi:
                                           (layer, page_id(bi, pi, pt_ref, gi) // PAGE_GROUP, 0, 0), gi=gi))
            for gi in range(g)]
    grid_spec = pltpu.PrefetchScalarGridSpec(
        num_scalar_prefetch=1, grid=(nb, npg // g),
        in_specs=[new, new, new, pl.BlockSpec((None, 1, h), lambda bi, pi, pt_ref: (bi, 0, 0))]
        + page + page + flat,
        out_specs=new,
        scratch_shapes=[pltpu.VMEM((h, 1), f32), pltpu.VMEM((h, 1), f32), pltpu.VMEM((h, dh), f32),
                        pltpu.VMEM((1, r * h), f32)])
    return pl.pallas_call(
        functools.partial(_decode_kernel, scale=dh ** -0.5, NPG=npg, G=g),
        out_shape=SDS((nb, h, dh), bf16), grid_spec=grid_spec,
        compiler_params=_cparams("parallel", "arbitrary"), name="decode_attention",
    )(pt, q, k_new, v_new, lf_new.reshape(nb, 1, h), *([cache_k] * g), *([cache_v] * g), *([logf_flat] * g))


def _disc_kernel(lr_ref, li_ref, ldt_ref, br_ref, bi_ref, lbr_ref, lbi_ref, bbr_ref, bbi_ref, cr_s, ci_s):
    lr = lr_ref[...]
    li = li_ref[...]
    dt = jnp.exp(ldt_ref[...])
    er = jnp.exp(lr * dt)
    ang = li * dt
    lbr = er * jnp.cos(ang)
    lbi = er * jnp.sin(ang)
    lbr_ref[...] = lbr
    lbi_ref[...] = lbi
    nr = lbr - 1.0
    den = lr * lr + li * li
    cr_s[...] = (nr * lr + lbi * li) / den
    ci_s[...] = (lbi * lr - nr * li) / den
    for g in range(lr.shape[0]):
        cr = cr_s[g:g + 1, :]
        ci = ci_s[g:g + 1, :]
        br = br_ref[g]
        bi = bi_ref[g]
        bbr_ref[g] = cr * br - ci * bi
        bbi_ref[g] = cr * bi + ci * br


def _discretize(lam_re, lam_im, log_dt, b_re, b_im):
    d, g, n = lam_re.shape
    p = b_re.shape[-1]
    dg = d * g
    bt = lambda x: jnp.transpose(x, (0, 1, 3, 2)).reshape(dg, p, n)
    lbr, lbi, bbr, bbi = pl.pallas_call(
        _disc_kernel,
        out_shape=[SDS((dg, n), f32), SDS((dg, n), f32), SDS((dg, p, n), f32), SDS((dg, p, n), f32)],
        scratch_shapes=[pltpu.VMEM((dg, n), f32), pltpu.VMEM((dg, n), f32)],
        compiler_params=_cparams(), name="s5_discretize",
    )(lam_re.reshape(dg, n), lam_im.reshape(dg, n), log_dt.reshape(dg, 1), bt(b_re), bt(b_im))
    return (lbr.reshape(d, g, n), lbi.reshape(d, g, n), bbr.reshape(d, g, p, n), bbi.reshape(d, g, p, n))


def _s5_layout(lbr, lbi, bbr, bbi, c_re, c_im):
    g, n = lbr.shape
    p = bbr.shape[1]
    gb = LANES // p
    nj = g // gb
    lam = jnp.concatenate([lbr.reshape(nj, gb * n), lbi.reshape(nj, gb * n)], axis=1)
    eye = jnp.eye(gb, dtype=f32)

    def in_block(bb):
        return jnp.einsum('jgpn,gh->jgphn', bb.reshape(nj, gb, p, n), eye).reshape(nj, gb * p, gb * n)

    def out_block(c):
        return jnp.einsum('jgpn,gh->jhngp', c.reshape(nj, gb, p, n), eye).reshape(nj, gb * n, gb * p)

    wb = jnp.concatenate([in_block(bbr), in_block(bbi)], axis=2).astype(bf16)
    wc = jnp.concatenate([out_block(c_re), -out_block(c_im)], axis=1).astype(bf16)
    return lam, wb, wc


def _gelu_tanh(x):
    return 0.5 * x * (1.0 + jnp.tanh(math.sqrt(2.0 / math.pi) * (x + 0.044715 * (x * x * x))))


def _s5_tail(y, wg_ref, bg_ref):
    yg = _gelu_tanh(y)
    gate = jnp.dot(yg.astype(bf16), wg_ref[...].astype(bf16), preferred_element_type=f32) + bg_ref[...]
    return yg * _sigmoid(gate)


def _s5_kernel(u_ref, h0_ref, lam_ref, wb_ref, wc_ref, d_ref, wg_ref, bg_ref, y_ref, hfin_ref,
               hs_ref, st_ref, *, T, NJ, NC):
    @pl.when(pl.program_id(1) == 0)
    def _():
        st_ref[...] = h0_ref[...]

    u = u_ref[...]
    ub = u.astype(bf16)
    for j in range(NJ):
        r = jnp.dot(ub[:, LANES * j:LANES * (j + 1)], wb_ref[j], preferred_element_type=f32)
        for c in range(NC):
            hs_ref[c, pl.ds(j, T, stride=NJ), :] = r[:, LANES * c:LANES * (c + 1)]
    half = NC // 2
    lam_r = [lam_ref[:, LANES * c:LANES * (c + 1)] for c in range(half)]
    lam_i = [lam_ref[:, LANES * (half + c):LANES * (half + c + 1)] for c in range(half)]

    def body(t, h):
        r0 = pl.multiple_of(t * NJ, NJ)
        new = [None] * NC
        for c in range(half):
            hr, hi = h[c], h[half + c]
            nr = lam_r[c] * hr - lam_i[c] * hi + hs_ref[c, pl.ds(r0, NJ), :]
            ni = lam_r[c] * hi + lam_i[c] * hr + hs_ref[half + c, pl.ds(r0, NJ), :]
            hs_ref[c, pl.ds(r0, NJ), :] = nr
            hs_ref[half + c, pl.ds(r0, NJ), :] = ni
            new[c], new[half + c] = nr, ni
        return tuple(new)

    h = lax.fori_loop(0, T, body, tuple(st_ref[:, LANES * c:LANES * (c + 1)] for c in range(NC)),
                      unroll=8 if T % 8 == 0 else 1)
    for c in range(NC):
        st_ref[:, LANES * c:LANES * (c + 1)] = h[c]
    hfin_ref[...] = st_ref[...]
    ys = []
    for j in range(NJ):
        hj = jnp.concatenate([hs_ref[c, pl.ds(j, T, stride=NJ), :] for c in range(NC)], axis=1)
        ys.append(jnp.dot(hj.astype(bf16), wc_ref[j], preferred_element_type=f32))
    y = jnp.concatenate(ys, axis=1) + d_ref[...] * u
    y_ref[...] = _s5_tail(y, wg_ref, bg_ref).astype(y_ref.dtype)


def _s5_prompt(u, h0, lam, wb, wc, d_skip, w_glu, b_glu, b, l):
    m, c = u.shape
    nj, _, sw = wb.shape
    nc = sw // LANES
    assert nj == SUBLANES and c == nj * LANES
    t = _tile(l, 512)
    nt = l // t
    const2 = lambda shape: pl.BlockSpec(shape, lambda bi, ti: (0, 0))
    const3 = lambda shape: pl.BlockSpec(shape, lambda bi, ti: (0, 0, 0))
    y, hfin = pl.pallas_call(
        functools.partial(_s5_kernel, T=t, NJ=nj, NC=nc),
        out_shape=[SDS((m, c), bf16), SDS((b, nj, sw), f32)], grid=(b, nt),
        in_specs=[pl.BlockSpec((t, c), lambda bi, ti: (bi * nt + ti, 0)),
                  pl.BlockSpec((None, nj, sw), lambda bi, ti: (bi, 0, 0)),
                  const2((nj, sw)), const3((nj, LANES, sw)), const3((nj, sw, LANES)),
                  const2((1, c)), const2((c, c)), const2((1, c))],
        out_specs=[pl.BlockSpec((t, c), lambda bi, ti: (bi * nt + ti, 0)),
                   pl.BlockSpec((None, nj, sw), lambda bi, ti: (bi, 0, 0))],
        scratch_shapes=[pltpu.VMEM((nc, t * nj, LANES), f32), pltpu.VMEM((nj, sw), f32)],
        compiler_params=_cparams("parallel", "arbitrary"), name="s5_prompt",
    )(u, h0, lam, wb, wc, d_skip.reshape(1, c), w_glu, b_glu.reshape(1, c))
    return y, hfin


def _s5_step_kernel(u_ref, h0_ref, lam_ref, wb_ref, wc_ref, d_ref, wg_ref, bg_ref, y_ref, hn_ref, *, NJ):
    u = u_ref[...]
    ub = u.astype(bf16)
    half = lam_ref.shape[1] // 2
    ys = []
    for j in range(NJ):
        r = jnp.dot(ub[:, LANES * j:LANES * (j + 1)], wb_ref[j], preferred_element_type=f32)
        lr, li = lam_ref[j:j + 1, 0:half], lam_ref[j:j + 1, half:]
        hr, hi = h0_ref[j, :, 0:half], h0_ref[j, :, half:]
        nr = lr * hr - li * hi + r[:, 0:half]
        ni = lr * hi + li * hr + r[:, half:]
        hn_ref[j, :, 0:half] = nr
        hn_ref[j, :, half:] = ni
        hj = jnp.concatenate([nr, ni], axis=1).astype(bf16)
        ys.append(jnp.dot(hj, wc_ref[j], preferred_element_type=f32))
    y = jnp.concatenate(ys, axis=1) + d_ref[...] * u
    y_ref[...] = _s5_tail(y, wg_ref, bg_ref).astype(y_ref.dtype)


def _s5_step(u, h0, lam, wb, wc, d_skip, w_glu, b_glu):
    nb, c = u.shape
    nj, _, sw = wb.shape
    return pl.pallas_call(
        functools.partial(_s5_step_kernel, NJ=nj),
        out_shape=[SDS((nb, c), bf16), SDS((nj, nb, sw), f32)],
        compiler_params=_cparams(), name="s5_step",
    )(u, h0, lam, wb, wc, d_skip.reshape(1, c), w_glu, b_glu.reshape(1, c))


def _merge_kernel(wa_ref, wo_ref, wy_ref, a_ref, o_ref, y_ref, g0_ref, g1_ref, g2_ref,
                  as_ref, os_ref, ys_ref, gs0_ref, gs1_ref, gs2_ref, m_ref, ms_ref):
    wa, wo, wy = wa_ref[...].astype(bf16), wo_ref[...].astype(bf16), wy_ref[...].astype(bf16)

    def mix(a, o, y, g0, g1, g2):
        ba = jnp.dot(a[...], wa, preferred_element_type=f32)
        bb = jnp.dot(o[...], wo, preferred_element_type=f32)
        bc = jnp.dot(y[...], wy, preferred_element_type=f32)
        return _sigmoid(g0[...]) * ba + _sigmoid(g1[...]) * bb + _sigmoid(g2[...]) * bc

    m_ref[...] = mix(a_ref, o_ref, y_ref, g0_ref, g1_ref, g2_ref).astype(m_ref.dtype)

    @pl.when(_first_row_block(False))
    def _():
        ms_ref[...] = mix(as_ref, os_ref, ys_ref, gs0_ref, gs1_ref, gs2_ref).astype(ms_ref.dtype)


def _merge(branches, side_branches, w_conv, w_attn, w_ssm, layer, gates, side_gates):
    a, o, y = branches
    m = a.shape[0]
    ms = side_branches[0].shape[0]
    d = w_conv.shape[2]
    tm = _tile(m, 1024)
    tn = _tile(d, 512)
    nb = d // tn
    lhs = lambda x: pl.BlockSpec((tm, x.shape[1]), lambda i, j: (i, 0), pipeline_mode=pl.Buffered(1))
    side = lambda x: pl.BlockSpec((ms, x.shape[1]), lambda i, j: (0, 0))
    wsp = lambda w: pl.BlockSpec((None, w.shape[1], tn), lambda i, j: (layer, 0, j))
    gate = lambda gi: pl.BlockSpec((tm, tn), lambda i, j: (i, j + gi * nb))
    sgate = lambda gi: pl.BlockSpec((ms, tn), lambda i, j: (0, j + gi * nb))
    return pl.pallas_call(
        _merge_kernel, out_shape=[SDS((m, d), bf16), SDS((ms, d), bf16)], grid=(m // tm, nb),
        in_specs=[wsp(w_conv), wsp(w_attn), wsp(w_ssm), lhs(a), lhs(o), lhs(y), gate(0), gate(1), gate(2)]
        + [side(x) for x in side_branches] + [sgate(0), sgate(1), sgate(2)],
        out_specs=[pl.BlockSpec((tm, tn), lambda i, j: (i, j)), _side_spec(ms, tn, nb, False)],
        compiler_params=_cparams("arbitrary", "arbitrary"), name="branch_merge",
    )(w_conv, w_attn, w_ssm, a, o, y, gates, gates, gates, *side_branches, side_gates, side_gates, side_gates)


def _resnorm_kernel(x_ref, t_ref, g_ref, g2_ref, xo_ref, h_ref):
    xn = x_ref[...] + _rms(t_ref[...], g_ref[...])
    xo_ref[...] = xn
    h_ref[...] = _rms(xn, g2_ref[...]).astype(h_ref.dtype)


def _resnorm_last_kernel(x_ref, t_ref, g_ref, xo_ref):
    xo_ref[...] = x_ref[...] + _rms(t_ref[...], g_ref[...])


def _resnorm(x, t, g, g_next=None):
    m, d = x.shape
    tm = _tile(m, 256)
    row = pl.BlockSpec((tm, d), lambda i: (i, 0))
    vec = pl.BlockSpec((1, d), lambda i: (0, 0))
    if g_next is None:
        return pl.pallas_call(
            _resnorm_last_kernel, out_shape=SDS((m, d), f32), grid=(m // tm,),
            in_specs=[row, row, vec], out_specs=row,
            compiler_params=_cparams("parallel"), name="resnorm_last")(x, t, g.reshape(1, d)), None
    return pl.pallas_call(
        _resnorm_kernel, out_shape=[SDS((m, d), f32), SDS((m, d), bf16)], grid=(m // tm,),
        in_specs=[row, row, vec, vec], out_specs=[row, row],
        compiler_params=_cparams("parallel"), name="resnorm")(x, t, g.reshape(1, d), g_next.reshape(1, d))


def _ffn_gu_kernel(h_ref, hs_ref, wg_ref, wu_ref, o_ref, os_ref):
    wg, wu = wg_ref[...].astype(bf16), wu_ref[...].astype(bf16)

    def swiglu(h):
        g = jnp.dot(h, wg, preferred_element_type=f32)
        u = jnp.dot(h, wu, preferred_element_type=f32)
        return g * _sigmoid(g) * u

    o_ref[...] = swiglu(h_ref[...]).astype(o_ref.dtype)

    @pl.when(_first_row_block(False))
    def _():
        os_ref[...] = swiglu(hs_ref[...]).astype(os_ref.dtype)


def _ffn_gate_up(h, hs, w_gate, w_up, layer):
    m, d = h.shape
    ms = hs.shape[0]
    f = w_gate.shape[2]
    tm = _tile(m, 2048)
    tn = _tile(f, 256)
    wsp = pl.BlockSpec((None, d, tn), lambda i, j: (layer, 0, j))
    return pl.pallas_call(
        _ffn_gu_kernel, out_shape=[SDS((m, f), bf16), SDS((ms, f), bf16)], grid=(m // tm, f // tn),
        in_specs=[pl.BlockSpec((tm, d), lambda i, j: (i, 0), pipeline_mode=pl.Buffered(1)),
                  pl.BlockSpec((ms, d), lambda i, j: (0, 0)), wsp, wsp],
        out_specs=[pl.BlockSpec((tm, tn), lambda i, j: (i, j)), _side_spec(ms, tn, f // tn, False)],
        compiler_params=_cparams("arbitrary", "arbitrary"), name="ffn_gate_up")(h, hs, w_gate, w_up)


def _run_layer(x, h, xs, hs, b, l, layer, p, big, s5p, conv_state, h0_s, past, g_next, kv_stack):
    bs = xs.shape[0]
    heads = p['b_forget'].shape[0]
    c_conv = p['conv_b'].shape[0]
    d_attn = big['w_branch_attn'].shape[1]
    dh = d_attn // heads
    c_ssm = p['w_glu'].shape[0]
    off_q = 2 * c_conv
    off_k = off_q + d_attn
    off_v = off_k + d_attn
    off_f = off_v + d_attn
    off_u = off_f + heads
    off_g = off_u + c_ssm
    w_in_t = big['w_in_t']

    depth, k_prev, v_prev = kv_stack
    proj = functools.partial(_mm, h, w_in_t, layer, xs=hs, nt=True)
    (zglu,), (zglu_s,) = proj([f32], off=0, n=off_q, name="in_glu")
    (q,), (q_s,) = proj([bf16], off=off_q, n=d_attn, name="in_q")
    (k_all, kb), (k_s, _) = proj([f32, bf16], off=off_k, n=d_attn, stack_first=(depth, k_prev), name="in_k")
    (v_all, vb), (v_s, _) = proj([f32, bf16], off=off_v, n=d_attn, stack_first=(depth, v_prev), name="in_v")
    (zf,), (zf_s,) = proj([f32], off=off_f, n=heads, name="in_f")
    (u,), (u_s,) = proj([f32], off=off_u, n=c_ssm, name="in_u")
    (gates,), (gates_s,) = proj([f32], off=off_g, tm=2048, weights_outer=False, name="in_gates")

    lam, wb, wc = s5p
    conv_args = (p['conv_w'], p['conv_b'], p['ln_conv_g'], p['ln_conv_b'])
    s5_args = (lam, wb, wc, p['ssm_d'].reshape(-1), p['w_glu'], p['b_glu'])
    n_groups, n_state = p['ssm_lambda_re'].shape
    nj = wb.shape[0]

    def unpack(hfin, nb):
        half = hfin.shape[2] // 2
        return hfin[:, :, :half].reshape(nb, n_groups, n_state), hfin[:, :, half:].reshape(nb, n_groups, n_state)

    conv0 = jnp.zeros((b, CONV_HIST, c_conv), f32)
    h0 = jnp.zeros((b, nj, 2 * n_groups * n_state // nj), f32)
    a, conv_new = _conv_prompt(zglu, conv0, *conv_args, b, l)
    conv_new = conv_new[:, CONV_HIST - (p['conv_w'].shape[0] - 1):, :]
    logf, cum = _logf_prompt(zf, p['b_forget'], b, l)
    o = _flash_prompt(q, kb, vb, cum, b, l, heads, dh)
    y, hfin = _s5_prompt(u, h0, *s5_args, b, l)
    state_p = (k_all, v_all, logf.reshape(b, l, heads), conv_new) + unpack(hfin, b)

    a_s, conv_new_s = _conv_step(zglu_s, conv_state, *conv_args)
    logf_s = _logf_step(zf_s, p['b_forget'])
    cache_k, cache_v, logf_flat, page_table = past
    o_s = _decode_attention(q_s.reshape(bs, heads, dh), k_s.reshape(bs, heads, dh), v_s.reshape(bs, heads, dh),
                            logf_s, cache_k, cache_v, logf_flat, page_table, layer).reshape(bs, d_attn)
    y_s, hn = _s5_step(u_s, jnp.transpose(h0_s, (1, 0, 2)), *s5_args)
    state_s = (k_s.reshape(bs, 1, heads, dh), v_s.reshape(bs, 1, heads, dh), logf_s.reshape(bs, 1, heads),
               conv_new_s) + unpack(jnp.transpose(hn, (1, 0, 2)), bs)

    mix, mix_s = _merge((a, o, y), (a_s, o_s, y_s), big['w_branch_conv'], big['w_branch_attn'],
                        big['w_branch_ssm'], layer, gates, gates_s)
    (t,), (t_s,) = _mm(mix, big['w_out'], layer, [f32], xs=mix_s, tm=2048, weights_outer=False, name="w_out")
    x1, hf = _resnorm(x, t, p['g_mix_post'], p['g_ffn_pre'])
    x1_s, hf_s = _resnorm(xs, t_s, p['g_mix_post'], p['g_ffn_pre'])
    act, act_s = _ffn_gate_up(hf, hf_s, big['w_ffn_gate'], big['w_ffn_up'], layer)
    (f,), (f_s,) = _mm(act, big['w_ffn_down'], layer, [f32], xs=act_s, tm=1024, tn=256, weights_outer=False,
                       name="ffn_down")
    x2, h_next = _resnorm(x1, f, p['g_ffn_post'], g_next)
    x2_s, h_next_s = _resnorm(x1_s, f_s, p['g_ffn_post'], g_next)
    return (x2, h_next, state_p), (x2_s, h_next_s, state_s)


def _pack_state(h_re, h_im, nj):
    b = h_re.shape[0]
    return jnp.concatenate([h_re.reshape(b, nj, -1), h_im.reshape(b, nj, -1)], axis=2)


def kernel(x_prompt, x_sample, cache_k, cache_v, cache_logf, state_conv, state_ssm_re, state_ssm_im, page_table, g_mix_pre, w_in, b_forget, conv_w, conv_b, ln_conv_g, ln_conv_b, ssm_lambda_re, ssm_lambda_im, ssm_log_dt, ssm_b_re, ssm_b_im, ssm_c_re, ssm_c_im, ssm_d, w_glu, b_glu, w_branch_conv, w_branch_attn, w_branch_ssm, w_out, g_mix_post, g_ffn_pre, w_ffn_gate, w_ffn_up, w_ffn_down, g_ffn_post):
    small = dict(g_mix_pre=g_mix_pre, b_forget=b_forget, conv_w=conv_w, conv_b=conv_b,
                 ln_conv_g=ln_conv_g, ln_conv_b=ln_conv_b, ssm_lambda_re=ssm_lambda_re, ssm_d=ssm_d,
                 w_glu=w_glu, b_glu=b_glu, g_mix_post=g_mix_post, g_ffn_pre=g_ffn_pre, g_ffn_post=g_ffn_post)
    big = dict(w_in_t=jnp.swapaxes(w_in, 1, 2), w_branch_conv=w_branch_conv, w_branch_attn=w_branch_attn,
               w_branch_ssm=w_branch_ssm, w_out=w_out, w_ffn_gate=w_ffn_gate, w_ffn_up=w_ffn_up,
               w_ffn_down=w_ffn_down.astype(bf16))
    depth = w_in.shape[0]
    bp, lp, d = x_prompt.shape
    bs, ls, _ = x_sample.shape
    assert ls == 1
    n_groups, n_state = ssm_lambda_re.shape[1:]
    p_ssm = ssm_b_re.shape[-1]
    nj = n_groups * p_ssm // LANES

    lbr, lbi, bbr, bbi = _discretize(ssm_lambda_re, ssm_lambda_im, ssm_log_dt, ssm_b_re, ssm_b_im)
    logf_flat = _flatten_logf(cache_logf)

    y_p = x_prompt.reshape(bp * lp, d)
    y_s = x_sample.reshape(bs * ls, d)
    h_p = _rmsnorm(y_p, g_mix_pre[0])
    h_s = _rmsnorm(y_s, g_mix_pre[0])
    outs_p, outs_s = [], []
    k_all = v_all = None
    for layer in range(depth):
        p = {name: val[layer] for name, val in small.items()}
        g_next = g_mix_pre[layer + 1] if layer + 1 < depth else None
        s5p = _s5_layout(lbr[layer], lbi[layer], bbr[layer], bbi[layer], ssm_c_re[layer], ssm_c_im[layer])
        past = (cache_k, cache_v, logf_flat, page_table)
        (y_p, h_p, state_p), (y_s, h_s, state_s) = _run_layer(
            y_p, h_p, y_s, h_s, bp, lp, layer, p, big, s5p, state_conv[layer],
            _pack_state(state_ssm_re[layer], state_ssm_im[layer], nj), past, g_next, (depth, k_all, v_all))
        k_all, v_all = state_p[0], state_p[1]
        outs_p.append(state_p)
        outs_s.append(state_s)
    stack = lambda outs, i: jnp.stack([o[i] for o in outs])
    heads = b_forget.shape[1]
    kv_shape = (depth, bp, lp, heads, k_all.shape[-1] // heads)
    return ((y_p.reshape(bp, lp, d), y_s.reshape(bs, ls, d), k_all.reshape(kv_shape), v_all.reshape(kv_shape))
            + tuple(stack(outs_p, i) for i in range(2, 6))
            + tuple(stack(outs_s, i) for i in range(6)))
```

```python
import functools
import math

import jax
import jax.numpy as jnp
from jax import lax
from jax.experimental import pallas as pl
from jax.experimental.pallas import tpu as pltpu

f32 = jnp.float32
bf16 = jnp.bfloat16
SDS = jax.ShapeDtypeStruct

LANES = 128
SUBLANES = 8
VMEM_LIMIT_BYTES = 56 * 2 ** 20
EPS = 1e-6
NEG_INF = -1e30
LOG2E = math.log2(math.e)
CONV_HIST = 32


def _cparams(*sem):
    return pltpu.CompilerParams(dimension_semantics=sem, vmem_limit_bytes=VMEM_LIMIT_BYTES)


def _sigmoid(x):
    return 1.0 / (1.0 + jnp.exp(-x))


def _tile(n, pref):
    if n <= pref:
        return n
    t = pref
    while n % t:
        t //= 2
    return t


def _rms(x, g):
    return x * lax.rsqrt(jnp.mean(x * x, axis=-1, keepdims=True) + EPS) * g


def _rmsnorm_kernel(x_ref, g_ref, o_ref):
    o_ref[...] = _rms(x_ref[...], g_ref[...]).astype(o_ref.dtype)


def _rmsnorm(x, g):
    m, d = x.shape
    tm = _tile(m, 256)
    return pl.pallas_call(
        _rmsnorm_kernel, out_shape=SDS((m, d), bf16), grid=(m // tm,),
        in_specs=[pl.BlockSpec((tm, d), lambda i: (i, 0)), pl.BlockSpec((1, d), lambda i: (0, 0))],
        out_specs=pl.BlockSpec((tm, d), lambda i: (i, 0)),
        compiler_params=_cparams("parallel"), name="rmsnorm")(x, g.reshape(1, d))


NT_DIMS = (((1,), (1,)), ((), ()))
ROW_ALIGN = 16


def _first_row_block(weights_outer):
    return pl.program_id(1 if weights_outer else 0) == 0


def _side_spec(ms, tn, n_blocks, weights_outer):
    if weights_outer:
        return pl.BlockSpec((ms, tn), lambda a, b: (0, a))
    return pl.BlockSpec((ms, tn), lambda a, b: (0, jnp.where(a == 0, b, n_blocks - 1)))


def _mm_kernel(x_ref, w_ref, *rest, n_out, nt, has_side, has_prev, weights_outer):
    xs_ref = rest[0] if has_side else None
    outs = rest[int(has_side) + int(has_prev):]
    o_refs, s_refs = outs[:n_out], outs[n_out:]
    w = (w_ref[0] if nt else w_ref[...]).astype(bf16)

    def mm(a):
        if nt:
            return lax.dot_general(a, w, NT_DIMS, preferred_element_type=f32)
        return jnp.dot(a, w, preferred_element_type=f32)

    acc = mm(x_ref[...])
    for o in o_refs:
        o[...] = acc.astype(o.dtype)
    if has_side:
        @pl.when(_first_row_block(weights_outer))
        def _():
            acc_s = mm(xs_ref[...])
            for o in s_refs:
                o[...] = acc_s.astype(o.dtype)


def _mm(x, w, layer, out_dtypes, *, xs=None, nt=False, off=0, n=None, tm=1024, tn=512, weights_outer=True,
        stack_first=None, name="mm"):
    m, k = x.shape
    n_total = w.shape[1] if nt else w.shape[2]
    n = n_total - off if n is None else n
    tm = _tile(m, tm)
    tn = _tile(n, tn)
    assert n % tn == 0
    if weights_outer:
        grid = (n // tn, m // tm)
        jj = lambda a, b: a
        ii = lambda a, b: b
        xspec = pl.BlockSpec((tm, k), lambda a, b: (b, 0))
    else:
        grid = (m // tm, n // tn)
        jj = lambda a, b: b
        ii = lambda a, b: a
        xspec = pl.BlockSpec((tm, k), lambda a, b: (a, 0), pipeline_mode=pl.Buffered(1))
    if nt:
        assert off % ROW_ALIGN == 0 and tn % ROW_ALIGN == 0
        wspec = pl.BlockSpec((pl.Element(1), pl.Element(tn), pl.Element(k)),
                             lambda a, b: (layer, pl.multiple_of(off + jj(a, b) * tn, ROW_ALIGN), 0))
    else:
        assert off % tn == 0
        wspec = pl.BlockSpec((None, k, tn), lambda a, b: (layer, 0, jj(a, b) + off // tn))
    n_out = len(out_dtypes)
    out_shape = [SDS((m, n), dt) for dt in out_dtypes]
    out_specs = [pl.BlockSpec((tm, tn), lambda a, b: (ii(a, b), jj(a, b))) for _ in out_dtypes]
    operands, in_specs, aliases = [x, w], [xspec, wspec], {}
    if xs is not None:
        ms = xs.shape[0]
        operands.append(xs)
        in_specs.append(pl.BlockSpec((ms, k), lambda a, b: (0, 0)))
        out_shape += [SDS((ms, n), dt) for dt in out_dtypes]
        out_specs += [_side_spec(ms, tn, n // tn, weights_outer) for _ in out_dtypes]
    if stack_first is not None:
        depth, prev = stack_first
        out_shape[0] = SDS((depth, m, n), out_dtypes[0])
        out_specs[0] = pl.BlockSpec((None, tm, tn), lambda a, b: (layer, ii(a, b), jj(a, b)))
        if prev is not None:
            aliases = {len(operands): 0}
            operands.append(prev)
            in_specs.append(pl.BlockSpec(memory_space=pl.ANY))
    outs = pl.pallas_call(
        functools.partial(_mm_kernel, n_out=n_out, nt=nt, has_side=xs is not None,
                          has_prev=bool(aliases), weights_outer=weights_outer),
        out_shape=out_shape, grid=grid, in_specs=in_specs, out_specs=out_specs,
        input_output_aliases=aliases,
        compiler_params=_cparams("arbitrary", "arbitrary"), name=name)(*operands)
    return outs if xs is None else (outs[:n_out], outs[n_out:])


def _conv_kernel(za_ref, zb_ref, prev_ref, w_ref, cb_ref, lg_ref, lb_ref, o_ref, new_ref,
                 ext_ref, conv_ref, *, T, W):
    t = pl.program_id(1)

    nc = cb_ref.shape[1]

    @pl.when(t == 0)
    def _():
        ext_ref[0:CONV_HIST, :] = prev_ref[...]
        ext_ref[CONV_HIST + T:CONV_HIST + T + SUBLANES, :] = jnp.zeros((SUBLANES, nc), f32)

    @pl.when(t > 0)
    def _():
        ext_ref[0:CONV_HIST, :] = ext_ref[T:T + CONV_HIST, :]

    ext_ref[CONV_HIST:CONV_HIST + T, :] = za_ref[...] * _sigmoid(zb_ref[...])
    rc = min(T, 64)
    base = CONV_HIST - (W - 1)
    for r0 in range(0, T, rc):
        acc = jnp.broadcast_to(cb_ref[...], (rc, nc))
        for s in range(SUBLANES):
            part = None
            for k in range(W):
                if (base + k) % SUBLANES != s:
                    continue
                a0 = r0 + base + k - s
                term = w_ref[k:k + 1, :] * ext_ref[a0:a0 + rc + (SUBLANES if s else 0), :]
                part = term if part is None else part + term
            if part is not None:
                acc = acc + part[s:s + rc, :]
        conv_ref[r0:r0 + rc, :] = acc
    c = conv_ref[...]
    mu = jnp.mean(c, axis=-1, keepdims=True)
    xc = c - mu
    y = xc * lax.rsqrt(jnp.mean(xc * xc, axis=-1, keepdims=True) + EPS) * lg_ref[...] + lb_ref[...]
    o_ref[...] = (y * _sigmoid(y)).astype(o_ref.dtype)
    new_ref[...] = ext_ref[T:T + CONV_HIST, :]


def _conv_prompt(zglu, prev32, conv_w, conv_b, ln_g, ln_b, b, l):
    m, c2 = zglu.shape
    c = c2 // 2
    w = conv_w.shape[0]
    t = _tile(l, 256)
    nt = l // t
    kern = functools.partial(_conv_kernel, T=t, W=w)
    row = lambda bi, ti: (bi * nt + ti, 0)
    vec = pl.BlockSpec((1, c), lambda bi, ti: (0, 0))
    return pl.pallas_call(
        kern, out_shape=[SDS((m, c), bf16), SDS((b, CONV_HIST, c), f32)], grid=(b, nt),
        in_specs=[pl.BlockSpec((t, c), row), pl.BlockSpec((t, c), lambda bi, ti: (bi * nt + ti, 1)),
                  pl.BlockSpec((None, CONV_HIST, c), lambda bi, ti: (bi, 0, 0)),
                  pl.BlockSpec((w, c), lambda bi, ti: (0, 0)), vec, vec, vec],
        out_specs=[pl.BlockSpec((t, c), row), pl.BlockSpec((None, CONV_HIST, c), lambda bi, ti: (bi, 0, 0))],
        scratch_shapes=[pltpu.VMEM((t + CONV_HIST + SUBLANES, c), f32), pltpu.VMEM((t, c), f32)],
        compiler_params=_cparams("parallel", "arbitrary"), name="conv_prompt",
    )(zglu, zglu, prev32, conv_w, conv_b.reshape(1, c), ln_g.reshape(1, c), ln_b.reshape(1, c))


def _conv_step_kernel(zg_ref, st_ref, w_ref, cb_ref, lg_ref, lb_ref, o_ref, new_ref, *, W):
    c = cb_ref.shape[1]
    nb = zg_ref.shape[0]
    a = zg_ref[:, 0:c] * _sigmoid(zg_ref[:, c:2 * c])
    rows = []
    for bi in range(nb):
        hist = st_ref[bi]
        conv = (jnp.sum(hist * w_ref[0:W - 1, :], axis=0, keepdims=True)
                + w_ref[W - 1:W, :] * a[bi:bi + 1, :] + cb_ref[...])
        rows.append(conv)
        new_ref[bi, 0:W - 2, :] = st_ref[bi, 1:W - 1, :]
        new_ref[bi, W - 2:W - 1, :] = a[bi:bi + 1, :]
    cv = jnp.concatenate(rows, axis=0)
    mu = jnp.mean(cv, axis=-1, keepdims=True)
    xc = cv - mu
    y = xc * lax.rsqrt(jnp.mean(xc * xc, axis=-1, keepdims=True) + EPS) * lg_ref[...] + lb_ref[...]
    o_ref[...] = (y * _sigmoid(y)).astype(o_ref.dtype)


def _conv_step(zglu, state, conv_w, conv_b, ln_g, ln_b):
    nb, c2 = zglu.shape
    c = c2 // 2
    w = conv_w.shape[0]
    return pl.pallas_call(
        functools.partial(_conv_step_kernel, W=w),
        out_shape=[SDS((nb, c), bf16), SDS((nb, w - 1, c), f32)],
        compiler_params=_cparams(), name="conv_step",
    )(zglu, state, conv_w, conv_b.reshape(1, c), ln_g.reshape(1, c), ln_b.reshape(1, c))


def _log_sigmoid(z):
    return jnp.minimum(z, 0.0) - jnp.log1p(jnp.exp(-jnp.abs(z)))


def _logf_kernel(zf_ref, b_ref, lf_ref, cum_ref, *, H):
    lf = _log_sigmoid(zf_ref[:, 0:H] + b_ref[...])
    lf_ref[...] = lf
    n = lf.shape[0]
    row = lax.broadcasted_iota(jnp.int32, lf.shape, 0)
    x = lf
    s = 1
    while s < n:
        x = x + jnp.where(row >= s, pltpu.roll(x, s, axis=0), 0.0)
        s *= 2
    cum_ref[...] = x


def _logf_prompt(zf, b_forget, b, l):
    h = b_forget.shape[0]
    return pl.pallas_call(
        functools.partial(_logf_kernel, H=h),
        out_shape=[SDS((b, l, h), f32), SDS((b, l, h), f32)], grid=(b,),
        in_specs=[pl.BlockSpec((l, zf.shape[1]), lambda i: (i, 0)), pl.BlockSpec((1, h), lambda i: (0, 0))],
        out_specs=[pl.BlockSpec((None, l, h), lambda i: (i, 0, 0))] * 2,
        compiler_params=_cparams("parallel"), name="logf_prompt")(zf, b_forget.reshape(1, h))


def _logf_step_kernel(zf_ref, b_ref, lf_ref, *, H):
    lf_ref[...] = _log_sigmoid(zf_ref[:, 0:H] + b_ref[...])


def _logf_step(zf, b_forget):
    h = b_forget.shape[0]
    return pl.pallas_call(functools.partial(_logf_step_kernel, H=h), out_shape=SDS((zf.shape[0], h), f32),
                          compiler_params=_cparams(), name="logf_step")(zf, b_forget.reshape(1, h))


def _flash_kernel(q_ref, k_ref, v_ref, cc_ref, cr_ref, o_ref, *, tq, scale, H):
    h = pl.program_id(1)
    n = q_ref.shape[0]
    nt = (((1,), (1,)), ((), ()))
    lane = lax.broadcasted_iota(jnp.int32, (n, H), 1)
    cq = jnp.sum(jnp.where(lane == h, cc_ref[...], 0.0), axis=1, keepdims=True) * LOG2E
    cr = cr_ref[...] * LOG2E
    scale2 = scale * LOG2E
    rr = lax.broadcasted_iota(jnp.int32, (tq, tq), 0)
    cc = lax.broadcasted_iota(jnp.int32, (tq, tq), 1)
    for i in range(n // tq):
        lo, hi = i * tq, (i + 1) * tq
        q = q_ref[lo:hi, :]
        cqi = cq[lo:hi, :]
        sd = lax.dot_general(q, k_ref[lo:hi, :], nt, preferred_element_type=f32) * scale2
        sd = sd + cqi - cr[:, lo:hi]
        sd = jnp.where(cc <= rr, sd, NEG_INF)
        m = jnp.max(sd, axis=1, keepdims=True)
        if i > 0:
            so = lax.dot_general(q, k_ref[0:lo, :], nt, preferred_element_type=f32) * scale2
            so = so + cqi - cr[:, 0:lo]
            m = jnp.maximum(m, jnp.max(so, axis=1, keepdims=True))
            po = jnp.exp2(so - m)
            den = jnp.sum(po, axis=1, keepdims=True)
            acc = jnp.dot(po.astype(bf16), v_ref[0:lo, :], preferred_element_type=f32)
        pd = jnp.exp2(sd - m)
        if i > 0:
            den = den + jnp.sum(pd, axis=1, keepdims=True)
            acc = acc + jnp.dot(pd.astype(bf16), v_ref[lo:hi, :], preferred_element_type=f32)
        else:
            den = jnp.sum(pd, axis=1, keepdims=True)
            acc = jnp.dot(pd.astype(bf16), v_ref[lo:hi, :], preferred_element_type=f32)
        o_ref[lo:hi, :] = (acc / den).astype(o_ref.dtype)


def _flash_prompt(q, k, v, cum, b, l, h, dh):
    m = q.shape[0]
    tq = _tile(l, 256)
    cum_row = jnp.transpose(cum, (0, 2, 1)).reshape(b, h, 1, l)
    blk = pl.BlockSpec((l, dh), lambda bi, hi: (bi, hi))
    return pl.pallas_call(
        functools.partial(_flash_kernel, tq=tq, scale=dh ** -0.5, H=h),
        out_shape=SDS((m, h * dh), bf16), grid=(b, h),
        in_specs=[blk, blk, blk,
                  pl.BlockSpec((None, l, h), lambda bi, hi: (bi, 0, 0)),
                  pl.BlockSpec((None, None, 1, l), lambda bi, hi: (bi, hi, 0, 0))],
        out_specs=blk,
        compiler_params=_cparams("parallel", "arbitrary"), name="flash_prompt")(q, k, v, cum, cum_row)


PAGE_GROUP = SUBLANES


def _decode_kernel(pt_ref, q_ref, kn_ref, vn_ref, lfn_ref, *rest, scale, NPG, G):
    k_refs, v_refs, f_refs = rest[0:G], rest[G:2 * G], rest[2 * G:3 * G]
    o_ref = rest[3 * G]
    m_s, l_s, acc_s, pref_s = rest[3 * G + 1:]
    bi = pl.program_id(0)
    p = pl.program_id(1)
    r, h, dh = k_refs[0].shape
    n = r * h

    @pl.when(p == 0)
    def _():
        m_s[...] = jnp.full(m_s.shape, NEG_INF, f32)
        l_s[...] = jnp.zeros(l_s.shape, f32)
        acc_s[...] = jnp.zeros(acc_s.shape, f32)
        pref_s[...] = jnp.zeros(pref_s.shape, f32)

    rows = []
    for g in range(G):
        pg = pt_ref[bi * NPG + p * G + g]
        rows.append(f_refs[g][pl.ds(jnp.bitwise_and(pg, PAGE_GROUP - 1), 1), :])
    lf = jnp.concatenate(rows, axis=0) if G > 1 else rows[0]
    lane = lax.broadcasted_iota(jnp.int32, lf.shape, 1)
    cw, tot = lf, lf
    s = h
    while s < n:
        cw = cw + jnp.where(lane >= s, pltpu.roll(cw, s, axis=1), 0.0)
        tot = tot + pltpu.roll(tot, s, axis=1)
        s *= 2
    off = pref_s[...]
    biases = []
    for g in range(G):
        biases.append(off + cw[g:g + 1, :])
        off = off + tot[g:g + 1, :]
    pref_s[...] = off
    bias = jnp.concatenate(biases, axis=1) if G > 1 else biases[0]

    qb = q_ref[...]
    sts =[lax.dot_general(qb, k_refs[g][...].reshape(n, dh).astype(bf16), NT_DIMS,
                           preferred_element_type=f32) for g in range(G)]
    st = (jnp.concatenate(sts, axis=1) if G > 1 else sts[0]) * scale
    s = st - bias
    lane = lax.broadcasted_iota(jnp.int32, s.shape, 1)
    sub = lax.broadcasted_iota(jnp.int32, s.shape, 0)
    s = jnp.where(jnp.bitwise_and(lane, h - 1) == sub, s, NEG_INF)
    m_old = m_s[...]
    m_new = jnp.maximum(m_old, jnp.max(s, axis=1, keepdims=True))
    alpha = jnp.exp(m_old - m_new)
    pr = jnp.exp(s - m_new)
    l_s[...] = alpha * l_s[...] + jnp.sum(pr, axis=1, keepdims=True)
    acc = alpha * acc_s[...]
    for g in range(G):
        acc = acc + jnp.dot(pr[:, g * n:(g + 1) * n].astype(bf16), v_refs[g][...].reshape(n, dh).astype(bf16),
                            preferred_element_type=f32)
    acc_s[...] = acc
    m_s[...] = m_new

    @pl.when(p == NPG // G - 1)
    def _():
        cn_row = pref_s[:, 0:h] + lfn_ref[...]
        eye = lax.broadcasted_iota(jnp.int32, (h, h), 0) == lax.broadcasted_iota(jnp.int32, (h, h), 1)
        cn = jnp.sum(jnp.where(eye, jnp.broadcast_to(cn_row, (h, h)), 0.0), axis=1, keepdims=True)
        s_new = jnp.sum(qb.astype(f32) * kn_ref[...], axis=1, keepdims=True) * scale - cn
        m_o = m_s[...]
        m_f = jnp.maximum(m_o, s_new)
        a = jnp.exp(m_o - m_f)
        pn = jnp.exp(s_new - m_f)
        den = a * l_s[...] + pn
        o_ref[...] = ((a * acc_s[...] + pn * vn_ref[...]) / den).astype(o_ref.dtype)


def _flatten_logf(cache_logf):
    d, n_pool, r, h = cache_logf.shape
    pad = -n_pool % PAGE_GROUP
    flat = cache_logf.reshape(d, n_pool, r * h)
    if pad:
        flat = jnp.pad(flat, ((0, 0), (0, pad), (0, 0)))
    return flat.reshape(d, (n_pool + pad) // PAGE_GROUP, PAGE_GROUP, r * h)


def _decode_attention(q, k_new, v_new, lf_new, cache_k, cache_v, logf_flat, page_table, layer):
    nb, h, dh = q.shape
    _, _, r, _, _ = cache_k.shape
    npg = page_table.shape[1]
    assert h & (h - 1) == 0 and h % SUBLANES == 0 and r & (r - 1) == 0
    g = max(c for c in (8, 4, 2, 1) if npg % c == 0)
    pt = page_table.reshape(-1).astype(jnp.int32)
    page_id = lambda bi, pi, pt_ref, gi: pt_ref[bi * npg + pi * g + gi]
    new = pl.BlockSpec((None, h, dh), lambda bi, pi, pt_ref: (bi, 0, 0))
    page = [pl.BlockSpec((None, None, r, h, dh),
                         functools.partial(lambda bi, pi, pt_ref, gi: (layer, page_id(bi, pi, pt_ref, gi), 0, 0, 0), gi=gi))
            for gi in range(g)]
    flat = [pl.BlockSpec((None, None, PAGE_GROUP, r * h),
                         functools.partial(lambda bi, pi, pt_ref, gi:
                                           (layer, page_id(bi, pi, pt_ref, gi) // PAGE_GROUP, 0, 0), gi=gi))
            for gi in range(g)]
    grid_spec = pltpu.PrefetchScalarGridSpec(
        num_scalar_prefetch=1, grid=(nb, npg // g),
        in_specs=[new, new, new, pl.BlockSpec((None, 1, h), lambda bi, pi, pt_ref: (bi, 0, 0))]
        + page + page + flat,
        out_specs=new,
        scratch_shapes=[pltpu.VMEM((h, 1), f32), pltpu.VMEM((h, 1), f32), pltpu.VMEM((h, dh), f32),
                        pltpu.VMEM((1, r * h), f32)])
    return pl.pallas_call(
        functools.partial(_decode_kernel, scale=dh ** -0.5, NPG=npg, G=g),
        out_shape=SDS((nb, h, dh), bf16), grid_spec=grid_spec,
        compiler_params=_cparams("parallel", "arbitrary"), name="decode_attention",
    )(pt, q, k_new, v_new, lf_new.reshape(nb, 1, h), *([cache_k] * g), *([cache_v] * g), *([logf_flat] * g))


def _disc_kernel(lr_ref, li_ref, ldt_ref, br_ref, bi_ref, lbr_ref, lbi_ref, bbr_ref, bbi_ref, cr_s, ci_s):
    lr = lr_ref[...]
    li = li_ref[...]
    dt = jnp.exp(ldt_ref[...])
    er = jnp.exp(lr * dt)
    ang = li * dt
    lbr = er * jnp.cos(ang)
    lbi = er * jnp.sin(ang)
    lbr_ref[...] = lbr
    lbi_ref[...] = lbi
    nr = lbr - 1.0
    den = lr * lr + li * li
    cr_s[...] = (nr * lr + lbi * li) / den
    ci_s[...] = (lbi * lr - nr * li) / den
    for g in range(lr.shape[0]):
        cr = cr_s[g:g + 1, :]
        ci = ci_s[g:g + 1, :]
        br = br_ref[g]
        bi = bi_ref[g]
        bbr_ref[g] = cr * br - ci * bi
        bbi_ref[g] = cr * bi + ci * br


def _discretize(lam_re, lam_im, log_dt, b_re, b_im):
    d, g, n = lam_re.shape
    p = b_re.shape[-1]
    dg = d * g
    bt = lambda x: jnp.transpose(x, (0, 1, 3, 2)).reshape(dg, p, n)
    lbr, lbi, bbr, bbi = pl.pallas_call(
        _disc_kernel,
        out_shape=[SDS((dg, n), f32), SDS((dg, n), f32), SDS((dg, p, n), f32), SDS((dg, p, n), f32)],
        scratch_shapes=[pltpu.VMEM((dg, n), f32), pltpu.VMEM((dg, n), f32)],
        compiler_params=_cparams(), name="s5_discretize",
    )(lam_re.reshape(dg, n), lam_im.reshape(dg, n), log_dt.reshape(dg, 1), bt(b_re), bt(b_im))
    return (lbr.reshape(d, g, n), lbi.reshape(d, g, n), bbr.reshape(d, g, p, n), bbi.reshape(d, g, p, n))


def _s5_layout(lbr, lbi, bbr, bbi, c_re, c_im):
    g, n = lbr.shape
    p = bbr.shape[1]
    gb = LANES // p
    nj = g // gb
    lam = jnp.concatenate([lbr.reshape(nj, gb * n), lbi.reshape(nj, gb * n)], axis=1)
    eye = jnp.eye(gb, dtype=f32)

    def in_block(bb):
        return jnp.einsum('jgpn,gh->jgphn', bb.reshape(nj, gb, p, n), eye).reshape(nj, gb * p, gb * n)

    def out_block(c):
        return jnp.einsum('jgpn,gh->jhngp', c.reshape(nj, gb, p, n), eye).reshape(nj, gb * n, gb * p)

    wb = jnp.concatenate([in_block(bbr), in_block(bbi)], axis=2).astype(bf16)
    wc = jnp.concatenate([out_block(c_re), -out_block(c_im)], axis=1).astype(bf16)
    return lam, wb, wc


def _gelu_tanh(x):
    return 0.5 * x * (1.0 + jnp.tanh(math.sqrt(2.0 / math.pi) * (x + 0.044715 * (x * x * x))))


def _s5_tail(y, wg_ref, bg_ref):
    yg = _gelu_tanh(y)
    gate = jnp.dot(yg.astype(bf16), wg_ref[...].astype(bf16), preferred_element_type=f32) + bg_ref[...]
    return yg * _sigmoid(gate)


def _s5_kernel(u_ref, h0_ref, lam_ref, wb_ref, wc_ref, d_ref, wg_ref, bg_ref, y_ref, hfin_ref,
               hs_ref, st_ref, *, T, NJ, NC):
    @pl.when(pl.program_id(1) == 0)
    def _():
        st_ref[...] = h0_ref[...]

    u = u_ref[...]
    ub = u.astype(bf16)
    for j in range(NJ):
        r = jnp.dot(ub[:, LANES * j:LANES * (j + 1)], wb_ref[j], preferred_element_type=f32)
        for c in range(NC):
            hs_ref[c, pl.ds(j, T, stride=NJ), :] = r[:, LANES * c:LANES * (c + 1)]
    half = NC // 2
    lam_r = [lam_ref[:, LANES * c:LANES * (c + 1)] for c in range(half)]
    lam_i = [lam_ref[:, LANES * (half + c):LANES * (half + c + 1)] for c in range(half)]

    def body(t, h):
        r0 = pl.multiple_of(t * NJ, NJ)
        new = [None] * NC
        for c in range(half):
            hr, hi = h[c], h[half + c]
            nr = lam_r[c] * hr - lam_i[c] * hi + hs_ref[c, pl.ds(r0, NJ), :]
            ni = lam_r[c] * hi + lam_i[c] * hr + hs_ref[half + c, pl.ds(r0, NJ), :]
            hs_ref[c, pl.ds(r0, NJ), :] = nr
            hs_ref[half + c, pl.ds(r0, NJ), :] = ni
            new[c], new[half + c] = nr, ni
        return tuple(new)

    h = lax.fori_loop(0, T, body, tuple(st_ref[:, LANES * c:LANES * (c + 1)] for c in range(NC)),
                      unroll=8 if T % 8 == 0 else 1)
    for c in range(NC):
        st_ref[:, LANES * c:LANES * (c + 1)] = h[c]
    hfin_ref[...] = st_ref[...]
    ys = []
    for j in range(NJ):
        hj = jnp.concatenate([hs_ref[c, pl.ds(j, T, stride=NJ), :] for c in range(NC)], axis=1)
        ys.append(jnp.dot(hj.astype(bf16), wc_ref[j], preferred_element_type=f32))
    y = jnp.concatenate(ys, axis=1) + d_ref[...] * u
    y_ref[...] = _s5_tail(y, wg_ref, bg_ref).astype(y_ref.dtype)


def _s5_prompt(u, h0, lam, wb, wc, d_skip, w_glu, b_glu, b, l):
    m, c = u.shape
    nj, _, sw = wb.shape
    nc = sw // LANES
    assert nj == SUBLANES and c == nj * LANES
    t = _tile(l, 512)
    nt = l // t
    const2 = lambda shape: pl.BlockSpec(shape, lambda bi, ti: (0, 0))
    const3 = lambda shape: pl.BlockSpec(shape, lambda bi, ti: (0, 0, 0))
    y, hfin = pl.pallas_call(
        functools.partial(_s5_kernel, T=t, NJ=nj, NC=nc),
        out_shape=[SDS((m, c), bf16), SDS((b, nj, sw), f32)], grid=(b, nt),
        in_specs=[pl.BlockSpec((t, c), lambda bi, ti: (bi * nt + ti, 0)),
                  pl.BlockSpec((None, nj, sw), lambda bi, ti: (bi, 0, 0)),
                  const2((nj, sw)), const3((nj, LANES, sw)), const3((nj, sw, LANES)),
                  const2((1, c)), const2((c, c)), const2((1, c))],
        out_specs=[pl.BlockSpec((t, c), lambda bi, ti: (bi * nt + ti, 0)),
                   pl.BlockSpec((None, nj, sw), lambda bi, ti: (bi, 0, 0))],
        scratch_shapes=[pltpu.VMEM((nc, t * nj, LANES), f32), pltpu.VMEM((nj, sw), f32)],
        compiler_params=_cparams("parallel", "arbitrary"), name="s5_prompt",
    )(u, h0, lam, wb, wc, d_skip.reshape(1, c), w_glu, b_glu.reshape(1, c))
    return y, hfin


def _s5_step_kernel(u_ref, h0_ref, lam_ref, wb_ref, wc_ref, d_ref, wg_ref, bg_ref, y_ref, hn_ref, *, NJ):
    u = u_ref[...]
    ub = u.astype(bf16)
    half = lam_ref.shape[1] // 2
    ys = []
    for j in range(NJ):
        r = jnp.dot(ub[:, LANES * j:LANES * (j + 1)], wb_ref[j], preferred_element_type=f32)
        lr, li = lam_ref[j:j + 1, 0:half], lam_ref[j:j + 1, half:]
        hr, hi = h0_ref[j, :, 0:half], h0_ref[j, :, half:]
        nr = lr * hr - li * hi + r[:, 0:half]
        ni = lr * hi + li * hr + r[:, half:]
        hn_ref[j, :, 0:half] = nr
        hn_ref[j, :, half:] = ni
        hj = jnp.concatenate([nr, ni], axis=1).astype(bf16)
        ys.append(jnp.dot(hj, wc_ref[j], preferred_element_type=f32))
    y = jnp.concatenate(ys, axis=1) + d_ref[...] * u
    y_ref[...] = _s5_tail(y, wg_ref, bg_ref).astype(y_ref.dtype)


def _s5_step(u, h0, lam, wb, wc, d_skip, w_glu, b_glu):
    nb, c = u.shape
    nj, _, sw = wb.shape
    return pl.pallas_call(
        functools.partial(_s5_step_kernel, NJ=nj),
        out_shape=[SDS((nb, c), bf16), SDS((nj, nb, sw), f32)],
        compiler_params=_cparams(), name="s5_step",
    )(u, h0, lam, wb, wc, d_skip.reshape(1, c), w_glu, b_glu.reshape(1, c))


def _merge_kernel(wa_ref, wo_ref, wy_ref, a_ref, o_ref, y_ref, g0_ref, g1_ref, g2_ref,
                  as_ref, os_ref, ys_ref, gs0_ref, gs1_ref, gs2_ref, m_ref, ms_ref):
    wa, wo, wy = wa_ref[...].astype(bf16), wo_ref[...].astype(bf16), wy_ref[...].astype(bf16)

    def mix(a, o, y, g0, g1, g2):
        ba = jnp.dot(a[...], wa, preferred_element_type=f32)
        bb = jnp.dot(o[...], wo, preferred_element_type=f32)
        bc = jnp.dot(y[...], wy, preferred_element_type=f32)
        return _sigmoid(g0[...]) * ba + _sigmoid(g1[...]) * bb + _sigmoid(g2[...]) * bc

    m_ref[...] = mix(a_ref, o_ref, y_ref, g0_ref, g1_ref, g2_ref).astype(m_ref.dtype)

    @pl.when(_first_row_block(False))
    def _():
        ms_ref[...] = mix(as_ref, os_ref, ys_ref, gs0_ref, gs1_ref, gs2_ref).astype(ms_ref.dtype)


def _merge(branches, side_branches, w_conv, w_attn, w_ssm, layer, gates, side_gates):
    a, o, y = branches
    m = a.shape[0]
    ms = side_branches[0].shape[0]
    d = w_conv.shape[2]
    tm = _tile(m, 1024)
    tn = _tile(d, 512)
    nb = d // tn
    lhs = lambda x: pl.BlockSpec((tm, x.shape[1]), lambda i, j: (i, 0), pipeline_mode=pl.Buffered(1))
    side = lambda x: pl.BlockSpec((ms, x.shape[1]), lambda i, j: (0, 0))
    wsp = lambda w: pl.BlockSpec((None, w.shape[1], tn), lambda i, j: (layer, 0, j))
    gate = lambda gi: pl.BlockSpec((tm, tn), lambda i, j: (i, j + gi * nb))
    sgate = lambda gi: pl.BlockSpec((ms, tn), lambda i, j: (0, j + gi * nb))
    return pl.pallas_call(
        _merge_kernel, out_shape=[SDS((m, d), bf16), SDS((ms, d), bf16)], grid=(m // tm, nb),
        in_specs=[wsp(w_conv), wsp(w_attn), wsp(w_ssm), lhs(a), lhs(o), lhs(y), gate(0), gate(1), gate(2)]
        + [side(x) for x in side_branches] + [sgate(0), sgate(1), sgate(2)],
        out_specs=[pl.BlockSpec((tm, tn), lambda i, j: (i, j)), _side_spec(ms, tn, nb, False)],
        compiler_params=_cparams("arbitrary", "arbitrary"), name="branch_merge",
    )(w_conv, w_attn, w_ssm, a, o, y, gates, gates, gates, *side_branches, side_gates, side_gates, side_gates)


def _resnorm_kernel(x_ref, t_ref, g_ref, g2_ref, xo_ref, h_ref):
    xn = x_ref[...] + _rms(t_ref[...], g_ref[...])
    xo_ref[...] = xn
    h_ref[...] = _rms(xn, g2_ref[...]).astype(h_ref.dtype)


def _resnorm_last_kernel(x_ref, t_ref, g_ref, xo_ref):
    xo_ref[...] = x_ref[...] + _rms(t_ref[...], g_ref[...])


def _resnorm(x, t, g, g_next=None):
    m, d = x.shape
    tm = _tile(m, 256)
    row = pl.BlockSpec((tm, d), lambda i: (i, 0))
    vec = pl.BlockSpec((1, d), lambda i: (0, 0))
    if g_next is None:
        return pl.pallas_call(
            _resnorm_last_kernel, out_shape=SDS((m, d), f32), grid=(m // tm,),
            in_specs=[row, row, vec], out_specs=row,
            compiler_params=_cparams("parallel"), name="resnorm_last")(x, t, g.reshape(1, d)), None
    return pl.pallas_call(
        _resnorm_kernel, out_shape=[SDS((m, d), f32), SDS((m, d), bf16)], grid=(m // tm,),
        in_specs=[row, row, vec, vec], out_specs=[row, row],
        compiler_params=_cparams("parallel"), name="resnorm")(x, t, g.reshape(1, d), g_next.reshape(1, d))


def _ffn_gu_kernel(h_ref, hs_ref, wg_ref, wu_ref, o_ref, os_ref):
    wg, wu = wg_ref[...].astype(bf16), wu_ref[...].astype(bf16)

    def swiglu(h):
        g = jnp.dot(h, wg, preferred_element_type=f32)
        u = jnp.dot(h, wu, preferred_element_type=f32)
        return g * _sigmoid(g) * u

    o_ref[...] = swiglu(h_ref[...]).astype(o_ref.dtype)

    @pl.when(_first_row_block(False))
    def _():
        os_ref[...] = swiglu(hs_ref[...]).astype(os_ref.dtype)


def _ffn_gate_up(h, hs, w_gate, w_up, layer):
    m, d = h.shape
    ms = hs.shape[0]
    f = w_gate.shape[2]
    tm = _tile(m, 2048)
    tn = _tile(f, 256)
    wsp = pl.BlockSpec((None, d, tn), lambda i, j: (layer, 0, j))
    return pl.pallas_call(
        _ffn_gu_kernel, out_shape=[SDS((m, f), bf16), SDS((ms, f), bf16)], grid=(m // tm, f // tn),
        in_specs=[pl.BlockSpec((tm, d), lambda i, j: (i, 0), pipeline_mode=pl.Buffered(1)),
                  pl.BlockSpec((ms, d), lambda i, j: (0, 0)), wsp, wsp],
        out_specs=[pl.BlockSpec((tm, tn), lambda i, j: (i, j)), _side_spec(ms, tn, f // tn, False)],
        compiler_params=_cparams("arbitrary", "arbitrary"), name="ffn_gate_up")(h, hs, w_gate, w_up)


def _run_layer(x, h, xs, hs, b, l, layer, p, big, s5p, conv_state, h0_s, past, g_next, kv_stack):
    bs = xs.shape[0]
    heads = p['b_forget'].shape[0]
    c_conv = p['conv_b'].shape[0]
    d_attn = big['w_branch_attn'].shape[1]
    dh = d_attn // heads
    c_ssm = p['w_glu'].shape[0]
    off_q = 2 * c_conv
    off_k = off_q + d_attn
    off_v = off_k + d_attn
    off_f = off_v + d_attn
    off_u = off_f + heads
    off_g = off_u + c_ssm
    w_in_t = big['w_in_t']

    depth, k_prev, v_prev = kv_stack
    proj = functools.partial(_mm, h, w_in_t, layer, xs=hs, nt=True)
    (zglu,), (zglu_s,) = proj([f32], off=0, n=off_q, name="in_glu")
    (q,), (q_s,) = proj([bf16], off=off_q, n=d_attn, name="in_q")
    (k_all, kb), (k_s, _) = proj([f32, bf16], off=off_k, n=d_attn, stack_first=None, name="in_k")
    (v_all, vb), (v_s, _) = proj([f32, bf16], off=off_v, n=d_attn, stack_first=None, name="in_v")
    (zf,), (zf_s,) = proj([f32], off=off_f, n=heads, name="in_f")
    (u,), (u_s,) = proj([f32], off=off_u, n=c_ssm, name="in_u")
    (gates,), (gates_s,) = proj([f32], off=off_g, tm=2048, weights_outer=False, name="in_gates")

    lam, wb, wc = s5p
    conv_args = (p['conv_w'], p['conv_b'], p['ln_conv_g'], p['ln_conv_b'])
    s5_args = (lam, wb, wc, p['ssm_d'].reshape(-1), p['w_glu'], p['b_glu'])
    n_groups, n_state = p['ssm_lambda_re'].shape
    nj = wb.shape[0]

    def unpack(hfin, nb):
        half = hfin.shape[2] // 2
        return hfin[:, :, :half].reshape(nb, n_groups, n_state), hfin[:, :, half:].reshape(nb, n_groups, n_state)

    conv0 = jnp.zeros((b, CONV_HIST, c_conv), f32)
    h0 = jnp.zeros((b, nj, 2 * n_groups * n_state // nj), f32)
    a, conv_new = _conv_prompt(zglu, conv0, *conv_args, b, l)
    conv_new = conv_new[:, CONV_HIST - (p['conv_w'].shape[0] - 1):, :]
    logf, cum = _logf_prompt(zf, p['b_forget'], b, l)
    o = _flash_prompt(q, kb, vb, cum, b, l, heads, dh)
    y, hfin = _s5_prompt(u, h0, *s5_args, b, l)
    state_p = (k_all, v_all, logf.reshape(b, l, heads), conv_new) + unpack(hfin, b)

    a_s, conv_new_s = _conv_step(zglu_s, conv_state, *conv_args)
    logf_s = _logf_step(zf_s, p['b_forget'])
    cache_k, cache_v, logf_flat, page_table = past
    o_s = _decode_attention(q_s.reshape(bs, heads, dh), k_s.reshape(bs, heads, dh), v_s.reshape(bs, heads, dh),
                            logf_s, cache_k, cache_v, logf_flat, page_table, layer).reshape(bs, d_attn)
    y_s, hn = _s5_step(u_s, jnp.transpose(h0_s, (1, 0, 2)), *s5_args)
    state_s = (k_s.reshape(bs, 1, heads, dh), v_s.reshape(bs, 1, heads, dh), logf_s.reshape(bs, 1, heads),
               conv_new_s) + unpack(jnp.transpose(hn, (1, 0, 2)), bs)

    mix, mix_s = _merge((a, o, y), (a_s, o_s, y_s), big['w_branch_conv'], big['w_branch_attn'],
                        big['w_branch_ssm'], layer, gates, gates_s)
    (t,), (t_s,) = _mm(mix, big['w_out'], layer, [f32], xs=mix_s, tm=2048, weights_outer=False, name="w_out")
    x1, hf = _resnorm(x, t, p['g_mix_post'], p['g_ffn_pre'])
    x1_s, hf_s = _resnorm(xs, t_s, p['g_mix_post'], p['g_ffn_pre'])
    act, act_s = _ffn_gate_up(hf, hf_s, big['w_ffn_gate'], big['w_ffn_up'], layer)
    (f,), (f_s,) = _mm(act, big['w_ffn_down'], layer, [f32], xs=act_s, tm=1024, tn=256, weights_outer=False,
                       name="ffn_down")
    x2, h_next = _resnorm(x1, f, p['g_ffn_post'], g_next)
    x2_s, h_next_s = _resnorm(x1_s, f_s, p['g_ffn_post'], g_next)
    return (x2, h_next, state_p), (x2_s, h_next_s, state_s)


def _pack_state(h_re, h_im, nj):
    b = h_re.shape[0]
    return jnp.concatenate([h_re.reshape(b, nj, -1), h_im.reshape(b, nj, -1)], axis=2)


def kernel(x_prompt, x_sample, cache_k, cache_v, cache_logf, state_conv, state_ssm_re, state_ssm_im, page_table, g_mix_pre, w_in, b_forget, conv_w, conv_b, ln_conv_g, ln_conv_b, ssm_lambda_re, ssm_lambda_im, ssm_log_dt, ssm_b_re, ssm_b_im, ssm_c_re, ssm_c_im, ssm_d, w_glu, b_glu, w_branch_conv, w_branch_attn, w_branch_ssm, w_out, g_mix_post, g_ffn_pre, w_ffn_gate, w_ffn_up, w_ffn_down, g_ffn_post):
    small = dict(g_mix_pre=g_mix_pre, b_forget=b_forget, conv_w=conv_w, conv_b=conv_b,
                 ln_conv_g=ln_conv_g, ln_conv_b=ln_conv_b, ssm_lambda_re=ssm_lambda_re, ssm_d=ssm_d,
                 w_glu=w_glu, b_glu=b_glu, g_mix_post=g_mix_post, g_ffn_pre=g_ffn_pre, g_ffn_post=g_ffn_post)
    big = dict(w_in_t=jnp.swapaxes(w_in, 1, 2), w_branch_conv=w_branch_conv, w_branch_attn=w_branch_attn,
               w_branch_ssm=w_branch_ssm, w_out=w_out, w_ffn_gate=w_ffn_gate, w_ffn_up=w_ffn_up,
               w_ffn_down=w_ffn_down.astype(bf16))
    depth = w_in.shape[0]
    bp, lp, d = x_prompt.shape
    bs, ls, _ = x_sample.shape
    assert ls == 1
    n_groups, n_state = ssm_lambda_re.shape[1:]
    p_ssm = ssm_b_re.shape[-1]
    nj = n_groups * p_ssm // LANES

    lbr, lbi, bbr, bbi = _discretize(ssm_lambda_re, ssm_lambda_im, ssm_log_dt, ssm_b_re, ssm_b_im)
    logf_flat = _flatten_logf(cache_logf)

    y_p = x_prompt.reshape(bp * lp, d)
    y_s = x_sample.reshape(bs * ls, d)
    h_p = _rmsnorm(y_p, g_mix_pre[0])
    h_s = _rmsnorm(y_s, g_mix_pre[0])
    outs_p, outs_s = [], []
    k_all = v_all = None
    for layer in range(depth):
        p = {name: val[layer] for name, val in small.items()}
        g_next = g_mix_pre[layer + 1] if layer + 1 < depth else None
        s5p = _s5_layout(lbr[layer], lbi[layer], bbr[layer], bbi[layer], ssm_c_re[layer], ssm_c_im[layer])
        past = (cache_k, cache_v, logf_flat, page_table)
        (y_p, h_p, state_p), (y_s, h_s, state_s) = _run_layer(
            y_p, h_p, y_s, h_s, bp, lp, layer, p, big, s5p, state_conv[layer],
            _pack_state(state_ssm_re[layer], state_ssm_im[layer], nj), past, g_next, (depth, k_all, v_all))
        k_all, v_all = state_p[0], state_p[1]
        outs_p.append(state_p)
        outs_s.append(state_s)
    stack = lambda outs, i: jnp.stack([o[i] for o in outs])
    heads = b_forget.shape[1]
    kv_shape = (depth, bp, lp, heads, k_all.shape[-1] // heads)
    return ((y_p.reshape(bp, lp, d), y_s.reshape(bs, ls, d),
             stack(outs_p, 0).reshape(kv_shape), stack(outs_p, 1).reshape(kv_shape))
            + tuple(stack(outs_p, i) for i in range(2, 6))
            + tuple(stack(outs_s, i) for i in range(6)))
```
